```python
import jax, jax.numpy as jnp
from jax import lax
import numpy as np

D_MODEL = 1024
BATCH = 4
SEQ = 8192
DEPTH = 1

CHUNK = 64
N_META = 16
D_MIX = D_MODEL
D_ATTN = D_MIX // 2
HEAD_DIM = 64
N_HEADS = D_ATTN // HEAD_DIM
D_POOL = D_MIX - D_ATTN
POOL_WINDOWS = (2, 4, 8, 16)
N_POOL_GROUPS = len(POOL_WINDOWS)
POOL_GROUP_DIM = D_POOL // N_POOL_GROUPS
D_IN = 3 * D_ATTN + N_HEADS + D_POOL
D_FF = ((8 * D_MODEL + 3 * 256 - 1) // (3 * 256)) * 256
Q_BLOCK = 128
EPS = 1e-6

kernel_name = "hymba_fox_poolformer_block"


def _rmsnorm(x, w):
    x32 = x.astype(jnp.float32)
    y = x32 * lax.rsqrt(jnp.mean(x32 * x32, axis=-1, keepdims=True) + EPS)
    return (y * w.astype(jnp.float32)).astype(x.dtype)


def _forgetting_attention(q, k, v, cum_logf):
    L = q.shape[2]
    scale = HEAD_DIM ** -0.5
    outs = []
    for q0 in range(0, L, Q_BLOCK):
        q1 = min(q0 + Q_BLOCK, L)
        qb = q[:, :, q0:q1]
        kp = k[:, :, :q1]
        vp = v[:, :, :q1]
        s = jnp.einsum("bhqd,bhkd->bhqk", qb, kp).astype(jnp.float32) * scale
        s = s + (cum_logf[:, :, q0:q1, None] - cum_logf[:, :, None, :q1])
        t_idx = jnp.arange(q0, q1)[:, None]
        s_idx = jnp.arange(q1)[None, :]
        s = jnp.where(s_idx <= t_idx, s, -jnp.inf)
        p = jax.nn.softmax(s, axis=-1).astype(v.dtype)
        outs.append(jnp.einsum("bhqk,bhkd->bhqd", p, vp))
    return jnp.concatenate(outs, axis=2)


def _trailing_mean_minus_self(u, window):
    L = u.shape[1]
    cs = jnp.concatenate([jnp.zeros_like(u[:, :1]), jnp.cumsum(u, axis=1)], axis=1)
    t = jnp.arange(L)
    lo = jnp.maximum(t + 1 - window, 0)
    total = cs[:, 1:] - cs[:, lo]
    count = (t + 1 - lo).astype(jnp.float32)[None, :, None]
    return total / count - u


def _layer(h, norm1_w, w_in, b_fgate, q_norm_w, k_norm_w, w_pool, pool_scale,
           w_out, norm2_w, w_gate, w_up, w_down):
    B, L, _ = h.shape
    n1 = _rmsnorm(h, norm1_w)
    proj = n1 @ w_in.astype(h.dtype)
    q, k, v, fg, u = jnp.split(
        proj, [D_ATTN, 2 * D_ATTN, 3 * D_ATTN, 3 * D_ATTN + N_HEADS], axis=-1)

    q = _rmsnorm(q.reshape(B, L, N_HEADS, HEAD_DIM), q_norm_w).transpose(0, 2, 1, 3)
    k = _rmsnorm(k.reshape(B, L, N_HEADS, HEAD_DIM), k_norm_w).transpose(0, 2, 1, 3)
    v = v.reshape(B, L, N_HEADS, HEAD_DIM).transpose(0, 2, 1, 3)
    log_f = jax.nn.log_sigmoid((fg + b_fgate.astype(fg.dtype)).astype(jnp.float32))
    cum_logf = jnp.cumsum(log_f.transpose(0, 2, 1), axis=-1)
    a_out = _forgetting_attention(q, k, v, cum_logf)
    a_out = a_out.transpose(0, 2, 1, 3).reshape(B, L, D_ATTN)

    u32 = u.astype(jnp.float32)
    pooled = jnp.stack(
        [_trailing_mean_minus_self(u32[..., g * POOL_GROUP_DIM:(g + 1) * POOL_GROUP_DIM], w)
         for g, w in enumerate(POOL_WINDOWS)], axis=2).astype(h.dtype)
    p_out = jnp.einsum("blgc,gcd->blgd", pooled, w_pool.astype(h.dtype)).reshape(B, L, D_POOL)
    p_out = p_out * pool_scale.astype(h.dtype)

    mix = jnp.concatenate([a_out, p_out], axis=-1) @ w_out.astype(h.dtype)
    h = h + mix

    n2 = _rmsnorm(h, norm2_w)
    ffn = (jax.nn.silu(n2 @ w_gate.astype(h.dtype)) * (n2 @ w_up.astype(h.dtype))) @ w_down.astype(h.dtype)
    return h + ffn


def setup_inputs(seed: int = 0) -> dict:
    key = jax.random.key(seed)
    ks = jax.random.split(key, 16)
    x = jax.random.normal(ks[0], (BATCH, SEQ, D_MODEL), jnp.float32)
    meta_tokens = jax.random.normal(ks[1], (N_META, D_MODEL), jnp.float32)
    norm1_w = 1.0 + 0.05 * jax.random.normal(ks[2], (DEPTH, D_MODEL), jnp.float32)
    w_in = jax.random.normal(ks[3], (DEPTH, D_MODEL, D_IN), jnp.float32) * D_MODEL ** -0.5
    w_in = w_in.at[:, :, 3 * D_ATTN:3 * D_ATTN + N_HEADS].multiply(0.1)
    b_fgate = 2.0 + 2.0 * jax.random.uniform(ks[4], (DEPTH, N_HEADS), jnp.float32)
    q_norm_w = 1.0 + 0.05 * jax.random.normal(ks[5], (DEPTH, HEAD_DIM), jnp.float32)
    k_norm_w = 1.0 + 0.05 * jax.random.normal(ks[6], (DEPTH, HEAD_DIM), jnp.float32)
    w_pool = jax.random.normal(ks[7], (DEPTH, N_POOL_GROUPS, POOL_GROUP_DIM, POOL_GROUP_DIM),
                               jnp.float32) * POOL_GROUP_DIM ** -0.5
    pool_scale = 1.0 + 0.1 * jax.random.normal(ks[8], (DEPTH, D_POOL), jnp.float32)
    w_out = jax.random.normal(ks[9], (DEPTH, D_MIX, D_MODEL), jnp.float32) * D_MIX ** -0.5
    norm2_w = 1.0 + 0.05 * jax.random.normal(ks[10], (DEPTH, D_MODEL), jnp.float32)
    w_gate = jax.random.normal(ks[11], (DEPTH, D_MODEL, D_FF), jnp.float32) * D_MODEL ** -0.5
    w_up = jax.random.normal(ks[12], (DEPTH, D_MODEL, D_FF), jnp.float32) * D_MODEL ** -0.5
    w_down = jax.random.normal(ks[13], (DEPTH, D_FF, D_MODEL), jnp.float32) * D_FF ** -0.5
    return {"x": x, "meta_tokens": meta_tokens, "norm1_w": norm1_w, "w_in": w_in,
            "b_fgate": b_fgate, "q_norm_w": q_norm_w, "k_norm_w": k_norm_w,
            "w_pool": w_pool, "pool_scale": pool_scale, "w_out": w_out,
            "norm2_w": norm2_w, "w_gate": w_gate, "w_up": w_up, "w_down": w_down}


def reference(x, meta_tokens, norm1_w, w_in, b_fgate, q_norm_w, k_norm_w, w_pool,
              pool_scale, w_out, norm2_w, w_gate, w_up, w_down):
    B = x.shape[0]
    meta = jnp.broadcast_to(meta_tokens.astype(x.dtype)[None], (B, N_META, D_MODEL))
    h = jnp.concatenate([meta, x], axis=1)
    for layer in range(DEPTH):
        h = _layer(h, norm1_w[layer], w_in[layer], b_fgate[layer], q_norm_w[layer],
                   k_norm_w[layer], w_pool[layer], pool_scale[layer], w_out[layer],
                   norm2_w[layer], w_gate[layer], w_up[layer], w_down[layer])
    return h[:, N_META:]
```

```python
import functools

import jax
import jax.numpy as jnp
from jax import lax
from jax.experimental import pallas as pl
from jax.experimental.pallas import tpu as pltpu

N_META = 16
N_HEADS = 8
HEAD_DIM = 64
D_ATTN = N_HEADS * HEAD_DIM
POOL_WINDOWS = (2, 4, 8, 16)
POOL_GROUP_DIM = 128
D_POOL = len(POOL_WINDOWS) * POOL_GROUP_DIM
EPS = 1e-6

LANES = 128
MXU_TILE = 256
HEADS_PER_STEP = LANES // HEAD_DIM
N_SPLIT = 3
GATE_ROWS = 16
ONES_ROWS = 16
MASKED = -1e30

META_BLOCK = 128
ROW_BLOCK = 512
Q_BLOCK = 512
KV_BLOCK = 512
FF_CHUNKS = (1024, 1024, 768)
VMEM_LIMIT_BYTES = 56 * 1024 * 1024


def _f32(x):
    return x.astype(jnp.float32)


def _bf16(x):
    return x.astype(jnp.bfloat16)


def _dot(a, b):
    return jnp.dot(a, b, preferred_element_type=jnp.float32)


def _dot_nt(a, b):
    return lax.dot_general(a, b, (((1,), (1,)), ((), ())), preferred_element_type=jnp.float32)


def _split_bf16(x, n):
    parts = []
    r = x
    for _ in range(n):
        p = _f32(_bf16(r))
        parts.append(p)
        r = r - p
    return parts


def _rms_scale(x):
    return lax.rsqrt(jnp.mean(x * x, axis=-1, keepdims=True) + EPS)


def _tile_lanes(x, n):
    return jnp.concatenate([x] * (n // x.shape[1]), axis=1) if n != x.shape[1] else x


def _norm_project(h, n1w_ref, wt_ref, wku_ref):
    n1 = _bf16(h * _rms_scale(h) * n1w_ref[...])
    proj_t = _dot_nt(wt_ref[...], n1)
    ku = _dot(n1, wku_ref[...])
    return proj_t, ku


def _key_norm(k, kw_ref, bd_ref):
    hi, lo = (_bf16(p) for p in _split_bf16(k * k, 2))
    bd = bd_ref[...]
    halves = []
    for c in range(D_ATTN // MXU_TILE):
        sl = slice(c * MXU_TILE, (c + 1) * MXU_TILE)
        halves.append(_dot(hi[:, sl], bd) + _dot(lo[:, sl], bd))
    ss = jnp.concatenate(halves, axis=1)
    return k * lax.rsqrt(ss * (1.0 / HEAD_DIM) + EPS) * kw_ref[...]


def _gate_prefix(fg_t, bfg_ref, tri_ref, carry, n_valid):
    t = fg_t.shape[1]
    logf = jax.nn.log_sigmoid(fg_t + _tile_lanes(bfg_ref[...], t))
    if n_valid < t:
        lane = lax.broadcasted_iota(jnp.int32, logf.shape, 1)
        logf = jnp.where(lane < n_valid, logf, 0.0)
    parts = _split_bf16(logf, N_SPLIT)
    pad = jnp.zeros((2 * GATE_ROWS - N_SPLIT * N_HEADS, t), jnp.float32)
    stacked = _bf16(jnp.concatenate(parts + [pad], axis=0))
    cs = _dot(stacked, tri_ref[...])
    c = _tile_lanes(carry, t) + cs[0:N_HEADS]
    for p in range(1, N_SPLIT):
        c = c + cs[p * N_HEADS:(p + 1) * N_HEADS]
    total = jnp.sum(logf, axis=1, keepdims=True)
    return c, total


def _key_bias_columns(c, n_valid):
    t = c.shape[1]
    parts = _split_bf16(-c, N_SPLIT)
    if n_valid < t:
        lane = lax.broadcasted_iota(jnp.int32, c.shape, 1)
        parts[0] = jnp.where(lane < n_valid, parts[0], MASKED)
    pad = jnp.zeros((LANES - N_SPLIT * N_HEADS, t), jnp.float32)
    return _bf16(jnp.concatenate(parts + [pad], axis=0).T)


def _meta_inproj_kernel(h_ref, n1w_ref, wt_ref, wku_ref, bfg_ref, kw_ref, bd_ref, tri_ref,
                        k_ref, cp_ref, vt_ref, u_ref, c_ref):
    proj_t, ku = _norm_project(h_ref[...], n1w_ref, wt_ref, wku_ref)
    vt_ref[...] = _bf16(proj_t[D_ATTN:2 * D_ATTN])
    k_ref[...] = _bf16(_key_norm(ku[:, 0:D_ATTN], kw_ref, bd_ref))
    u_ref[...] = ku[0:N_META, D_ATTN:]
    zero = jnp.zeros((N_HEADS, LANES), jnp.float32)
    c, total = _gate_prefix(proj_t[2 * D_ATTN:2 * D_ATTN + N_HEADS], bfg_ref, tri_ref, zero, N_META)
    cp_ref[...] = _key_bias_columns(c, N_META)
    c_ref[...] = jnp.broadcast_to(total, (N_HEADS, LANES))


def _inproj_kernel(x_ref, um_ref, cm_ref, n1w_ref, wt_ref, wku_ref, bfg_ref, qw_ref, kw_ref, bd_ref,
                   tri_ref, wp_ref, ps_ref,
                   qt_ref, k_ref, cp_ref, vt_ref, po_ref,
                   ubuf_ref, carry_ref):
    t = x_ref.shape[0]

    @pl.when(pl.program_id(1) == 0)
    def _():
        ubuf_ref[0:N_META, :] = um_ref[...]
        carry_ref[...] = cm_ref[...]

    proj_t, ku = _norm_project(x_ref[...], n1w_ref, wt_ref, wku_ref)
    vt_ref[...] = _bf16(proj_t[D_ATTN:2 * D_ATTN])

    q3 = proj_t[0:D_ATTN].reshape(N_HEADS, HEAD_DIM, t)
    q3 = q3 * lax.rsqrt(jnp.mean(q3 * q3, axis=1, keepdims=True) + EPS)
    qt_ref[...] = _bf16(q3.reshape(D_ATTN, t) * _tile_lanes(qw_ref[...], t))

    k_ref[...] = _bf16(_key_norm(ku[:, 0:D_ATTN], kw_ref, bd_ref))

    carry = carry_ref[...]
    c, total = _gate_prefix(proj_t[2 * D_ATTN:2 * D_ATTN + N_HEADS], bfg_ref, tri_ref, carry, t)
    cp_ref[...] = _key_bias_columns(c, t)
    carry_ref[...] = carry + total

    ubuf_ref[N_META:, :] = ku[:, D_ATTN:]
    groups = []
    for g, w in enumerate(POOL_WINDOWS):
        lanes = slice(g * POOL_GROUP_DIM, (g + 1) * POOL_GROUP_DIM)
        cur = ubuf_ref[N_META:, lanes]
        acc = cur
        for j in range(1, w):
            acc = acc + ubuf_ref[N_META - j:N_META - j + t, lanes]
        groups.append(acc * (1.0 / w) - cur)
    pooled = _bf16(jnp.concatenate(groups, axis=1))
    ubuf_ref[0:N_META, :] = ubuf_ref[t:t + N_META, :]
    halves = []
    for c2 in range(D_POOL // MXU_TILE):
        halves.append(_dot(pooled[:, c2 * MXU_TILE:(c2 + 1) * MXU_TILE], wp_ref[c2]))
    po_ref[...] = _bf16(jnp.concatenate(halves, axis=1) * ps_ref[...])


def _attn_kernel(qt_ref, k_ref, cp_ref, vt_ref, km_ref, cpm_ref, vtm_ref, o_ref, acc_ref):
    pair = pl.program_id(1)
    qi = pl.program_id(2)
    bq = qt_ref.shape[1]
    bk = KV_BLOCK
    ones = jnp.ones((ONES_ROWS, bk), jnp.bfloat16)

    qt = qt_ref[...]
    zeros_q = jnp.zeros((HEAD_DIM, bq), qt.dtype)
    row = lax.broadcasted_iota(jnp.int32, (LANES, bq), 0)
    q_aug = []
    for hh in range(HEADS_PER_STEP):
        head = pair * HEADS_PER_STEP + hh
        pieces = [qt[r * HEAD_DIM:(r + 1) * HEAD_DIM] if r == hh else zeros_q
                  for r in range(HEADS_PER_STEP)]
        sel = ((row & (N_HEADS - 1)) == head) & (row < N_SPLIT * N_HEADS)
        q_aug.append(jnp.concatenate(pieces + [jnp.where(sel, 1.0, 0.0).astype(qt.dtype)], axis=0))

    def scores(k_aug, hh, mask):
        s = _dot(k_aug, q_aug[hh])
        return s if mask is None else jnp.where(mask, s, MASKED)

    def v_aug(v_blk, hh):
        return jnp.concatenate([v_blk[hh * HEAD_DIM:(hh + 1) * HEAD_DIM],
                                ones[:, 0:v_blk.shape[1]]], axis=0)

    km_aug = jnp.concatenate([km_ref[...], cpm_ref[...]], axis=1)
    vtm = vtm_ref[...]
    ms = []
    for hh in range(HEADS_PER_STEP):
        s = scores(km_aug, hh, None)
        m = jnp.max(s, axis=0, keepdims=True)
        acc_ref[hh] = _dot(v_aug(vtm, hh), _bf16(jnp.exp(s - m)))
        ms.append(m)

    def tile(n, ms, mask):
        start = pl.multiple_of(n * bk, bk)
        k_aug = jnp.concatenate([k_ref[pl.ds(start, bk), :], cp_ref[pl.ds(start, bk), :]], axis=1)
        v_blk = vt_ref[:, pl.ds(start, bk)]
        out = []
        for hh in range(HEADS_PER_STEP):
            s = scores(k_aug, hh, mask)
            m_new = jnp.maximum(ms[hh], jnp.max(s, axis=0, keepdims=True))
            p = _bf16(jnp.exp(s - m_new))
            acc_ref[hh] = jnp.exp(ms[hh] - m_new) * acc_ref[hh] + _dot(v_aug(v_blk, hh), p)
            out.append(m_new)
        return tuple(out)

    ms = lax.fori_loop(0, qi, lambda n, ms: tile(n, ms, None), tuple(ms))
    causal = (lax.broadcasted_iota(jnp.int32, (bk, bq), 0)
              <= lax.broadcasted_iota(jnp.int32, (bk, bq), 1))
    tile(qi, ms, causal)

    outs = []
    for hh in range(HEADS_PER_STEP):
        acc = acc_ref[hh]
        outs.append(acc[0:HEAD_DIM] / acc[HEAD_DIM:HEAD_DIM + 1])
    o_ref[...] = _bf16(jnp.concatenate(outs, axis=0).T)


def _out_ffn_kernel(x_ref, a_ref, p_ref, wo_ref, n2w_ref, wg_ref, wu_ref, wd_ref, o_ref):
    mix = _dot(jnp.concatenate([a_ref[...], p_ref[...]], axis=1), wo_ref[...])
    h1 = x_ref[...] + mix
    n2 = _bf16(h1 * _rms_scale(h1) * n2w_ref[...])
    acc = h1
    lo = 0
    for width in FF_CHUNKS:
        g = _dot(n2, wg_ref[:, lo:lo + width])
        u = _dot(n2, wu_ref[:, lo:lo + width])
        acc = acc + _dot(_bf16(g * jax.nn.sigmoid(g) * u), wd_ref[lo:lo + width, :])
        lo += width
    o_ref[...] = acc


def _const_spec(shape, single_buffer=False):
    zeros = (0,) * len(shape)
    if single_buffer:
        return pl.BlockSpec(shape, lambda *_: zeros, pipeline_mode=pl.Buffered(1))
    return pl.BlockSpec(shape, lambda *_: zeros)


def _block_diag(blocks):
    n, r, c = blocks.shape
    out = jnp.zeros((n * r, n * c), blocks.dtype)
    for i in range(n):
        out = out.at[i * r:(i + 1) * r, i * c:(i + 1) * c].set(blocks[i])
    return out


def kernel(x, meta_tokens, norm1_w, w_in, b_fgate, q_norm_w, k_norm_w, w_pool, pool_scale, w_out,
           norm2_w, w_gate, w_up, w_down):
    batch, seq, d_model = x.shape
    assert w_in.shape[0] == 1, "one layer"
    assert seq % ROW_BLOCK == 0 and seq % Q_BLOCK == 0 and Q_BLOCK == KV_BLOCK
    d_ff = w_gate.shape[-1]
    assert sum(FF_CHUNKS) == d_ff
    f32, bf16 = jnp.float32, jnp.bfloat16

    w = w_in[0]
    w_q, w_k, w_v = (w[:, i * D_ATTN:(i + 1) * D_ATTN] for i in range(3))
    w_fg = w[:, 3 * D_ATTN:3 * D_ATTN + N_HEADS]
    w_u = w[:, 3 * D_ATTN + N_HEADS:]
    wt = jnp.concatenate([w_q.T, w_v.T, w_fg.T, jnp.zeros((GATE_ROWS - N_HEADS, d_model), f32)],
                         axis=0).astype(bf16)
    wku = jnp.concatenate([w_k, w_u], axis=1).astype(bf16)
    n1w = norm1_w[0].reshape(1, d_model)
    n2w = norm2_w[0].reshape(1, d_model)
    bfg = jnp.broadcast_to(b_fgate[0].reshape(N_HEADS, 1), (N_HEADS, LANES))
    qw = jnp.broadcast_to(jnp.tile(q_norm_w[0] * HEAD_DIM ** -0.5, N_HEADS).reshape(D_ATTN, 1),
                          (D_ATTN, LANES))
    kw = jnp.tile(k_norm_w[0], N_HEADS).reshape(1, D_ATTN)
    bd = _block_diag(jnp.ones((MXU_TILE // HEAD_DIM, HEAD_DIM, HEAD_DIM), bf16))
    wp = jnp.stack([_block_diag(w_pool[0, 2 * i:2 * i + 2]) for i in range(D_POOL // MXU_TILE)]
                   ).astype(bf16)
    ps = pool_scale[0].reshape(1, D_POOL)
    wo = w_out[0].astype(bf16)
    wg = w_gate[0].astype(bf16)
    wu = w_up[0].astype(bf16)
    wd = w_down[0].astype(bf16)

    def upper_tri(n):
        return (lax.broadcasted_iota(jnp.int32, (n, n), 0)
                <= lax.broadcasted_iota(jnp.int32, (n, n), 1)).astype(bf16)

    meta_blk = jnp.zeros((META_BLOCK, d_model), f32).at[0:N_META].set(meta_tokens)
    n_proj_t = 2 * D_ATTN + GATE_ROWS

    km, cpm, vtm, um, cm = pl.pallas_call(
        _meta_inproj_kernel,
        out_shape=(jax.ShapeDtypeStruct((META_BLOCK, D_ATTN), bf16),
                   jax.ShapeDtypeStruct((META_BLOCK, LANES), bf16),
                   jax.ShapeDtypeStruct((D_ATTN, META_BLOCK), bf16),
                   jax.ShapeDtypeStruct((N_META, D_POOL), f32),
                   jax.ShapeDtypeStruct((N_HEADS, LANES), f32)),
        compiler_params=pltpu.CompilerParams(vmem_limit_bytes=VMEM_LIMIT_BYTES),
        name="meta_inproj",
    )(meta_blk, n1w, wt, wku, bfg, kw, bd, upper_tri(META_BLOCK))

    n_row_blocks = seq // ROW_BLOCK
    tok = lambda b, j: (b, j, 0)
    tok_t = lambda b, j: (b, 0, j)
    qt, k, cp, vt, po = pl.pallas_call(
        _inproj_kernel,
        grid=(batch, n_row_blocks),
        in_specs=[pl.BlockSpec((None, ROW_BLOCK, d_model), tok),
                  _const_spec((N_META, D_POOL)), _const_spec((N_HEADS, LANES)),
                  _const_spec((1, d_model)), _const_spec((n_proj_t, d_model)),
                  _const_spec((d_model, D_ATTN + D_POOL)), _const_spec((N_HEADS, LANES)),
                  _const_spec((D_ATTN, LANES)), _const_spec((1, D_ATTN)),
                  _const_spec((MXU_TILE, MXU_TILE)), _const_spec((ROW_BLOCK, ROW_BLOCK)),
                  _const_spec((D_POOL // MXU_TILE, MXU_TILE, MXU_TILE)), _const_spec((1, D_POOL))],
        out_specs=(pl.BlockSpec((None, D_ATTN, ROW_BLOCK), tok_t),
                   pl.BlockSpec((None, ROW_BLOCK, D_ATTN), tok),
                   pl.BlockSpec((None, ROW_BLOCK, LANES), tok),
                   pl.BlockSpec((None, D_ATTN, ROW_BLOCK), tok_t),
                   pl.BlockSpec((None, ROW_BLOCK, D_POOL), tok)),
        out_shape=(jax.ShapeDtypeStruct((batch, D_ATTN, seq), bf16),
                   jax.ShapeDtypeStruct((batch, seq, D_ATTN), bf16),
                   jax.ShapeDtypeStruct((batch, seq, LANES), bf16),
                   jax.ShapeDtypeStruct((batch, D_ATTN, seq), bf16),
                   jax.ShapeDtypeStruct((batch, seq, D_POOL), bf16)),
        scratch_shapes=[pltpu.VMEM((N_META + ROW_BLOCK, D_POOL), f32),
                        pltpu.VMEM((N_HEADS, LANES), f32)],
        compiler_params=pltpu.CompilerParams(
            dimension_semantics=("arbitrary", "arbitrary"), vmem_limit_bytes=VMEM_LIMIT_BYTES),
        name="inproj",
    )(x, um, cm, n1w, wt, wku, bfg, qw, kw, bd, upper_tri(ROW_BLOCK), wp, ps)

    n_pairs = N_HEADS // HEADS_PER_STEP
    a = pl.pallas_call(
        _attn_kernel,
        grid=(batch, n_pairs, seq // Q_BLOCK),
        in_specs=[pl.BlockSpec((None, LANES, Q_BLOCK), lambda b, p, i: (b, p, i)),
                  pl.BlockSpec((None, seq, LANES), lambda b, p, i: (b, 0, p)),
                  pl.BlockSpec((None, seq, LANES), lambda b, p, i: (b, 0, 0)),
                  pl.BlockSpec((None, LANES, seq), lambda b, p, i: (b, p, 0)),
                  pl.BlockSpec((META_BLOCK, LANES), lambda b, p, i: (0, p)),
                  pl.BlockSpec((META_BLOCK, LANES), lambda b, p, i: (0, 0)),
                  pl.BlockSpec((LANES, META_BLOCK), lambda b, p, i: (p, 0))],
        out_specs=pl.BlockSpec((None, Q_BLOCK, LANES), lambda b, p, i: (b, i, p)),
        out_shape=jax.ShapeDtypeStruct((batch, seq, D_ATTN), bf16),
        scratch_shapes=[pltpu.VMEM((HEADS_PER_STEP, HEAD_DIM + ONES_ROWS, Q_BLOCK), f32)],
        compiler_params=pltpu.CompilerParams(
            dimension_semantics=("parallel", "parallel", "arbitrary"),
            vmem_limit_bytes=VMEM_LIMIT_BYTES),
        name="fox_attn",
    )(qt, k, cp, vt, km, cpm, vtm)

    rows = batch * seq
    row = lambda r: (r, 0)
    out = pl.pallas_call(
        _out_ffn_kernel,
        grid=(rows // ROW_BLOCK,),
        in_specs=[pl.BlockSpec((ROW_BLOCK, d_model), row),
                  pl.BlockSpec((ROW_BLOCK, D_ATTN), row),
                  pl.BlockSpec((ROW_BLOCK, D_POOL), row),
                  _const_spec((D_ATTN + D_POOL, d_model), True), _const_spec((1, d_model)),
                  _const_spec((d_model, d_ff), True), _const_spec((d_model, d_ff), True),
                  _const_spec((d_ff, d_model), True)],
        out_specs=pl.BlockSpec((ROW_BLOCK, d_model), row),
        out_shape=jax.ShapeDtypeStruct((rows, d_model), f32),
        compiler_params=pltpu.CompilerParams(
            dimension_semantics=("parallel",), vmem_limit_bytes=VMEM_LIMIT_BYTES),
        name="out_ffn",
    )(x.reshape(rows, d_model), a.reshape(rows, D_ATTN), po.reshape(rows, D_POOL),
      wo, n2w, wg, wu, wd)
    return out.reshape(batch, seq, d_model)
```

```python
import functools

import jax
import jax.numpy as jnp
from jax import lax
from jax.experimental import pallas as pl
from jax.experimental.pallas import tpu as pltpu

N_META = 16
N_HEADS = 8
HEAD_DIM = 64
D_ATTN = N_HEADS * HEAD_DIM
POOL_WINDOWS = (2, 4, 8, 16)
POOL_GROUP_DIM = 128
D_POOL = len(POOL_WINDOWS) * POOL_GROUP_DIM
EPS = 1e-6

LANES = 128
MXU_TILE = 256
HEADS_PER_STEP = LANES // HEAD_DIM
N_SPLIT = 3
GATE_ROWS = 16
ONES_ROWS = 16
MASKED = -1e30

META_BLOCK = 128
ROW_BLOCK = 512
Q_BLOCK = 512
KV_BLOCK = 512
FF_CHUNKS = (1024, 1024, 768)
VMEM_LIMIT_BYTES = 56 * 1024 * 1024


def _f32(x):
    return x.astype(jnp.float32)


def _bf16(x):
    return x.astype(jnp.bfloat16)


def _dot(a, b):
    return jnp.dot(a, b, preferred_element_type=jnp.float32)


def _dot_nt(a, b):
    return lax.dot_general(a, b, (((1,), (1,)), ((), ())), preferred_element_type=jnp.float32)


def _split_bf16(x, n):
    parts = []
    r = x
    for _ in range(n):
        p = _f32(_bf16(r))
        parts.append(p)
        r = r - p
    return parts


def _rms_scale(x):
    return lax.rsqrt(jnp.mean(x * x, axis=-1, keepdims=True) + EPS)


def _tile_lanes(x, n):
    return jnp.concatenate([x] * (n // x.shape[1]), axis=1) if n != x.shape[1] else x


def _norm_project(h, n1w_ref, wt_ref, wku_ref):
    n1 = _bf16(h * _rms_scale(h) * n1w_ref[...])
    proj_t = _dot_nt(wt_ref[...], n1)
    ku = _dot(n1, wku_ref[...])
    return proj_t, ku


def _key_norm(k, kw_ref, bd_ref):
    hi, lo = (_bf16(p) for p in _split_bf16(k * k, 2))
    bd = bd_ref[...]
    halves = []
    for c in range(D_ATTN // MXU_TILE):
        sl = slice(c * MXU_TILE, (c + 1) * MXU_TILE)
        halves.append(_dot(hi[:, sl], bd) + _dot(lo[:, sl], bd))
    ss = jnp.concatenate(halves, axis=1)
    return k * lax.rsqrt(ss * (1.0 / HEAD_DIM) + EPS) * kw_ref[...]


def _gate_prefix(fg_t, bfg_ref, tri_ref, carry, n_valid):
    t = fg_t.shape[1]
    logf = jax.nn.log_sigmoid(fg_t + _tile_lanes(bfg_ref[...], t))
    if n_valid < t:
        lane = lax.broadcasted_iota(jnp.int32, logf.shape, 1)
        logf = jnp.where(lane < n_valid, logf, 0.0)
    parts = _split_bf16(logf, N_SPLIT)
    pad = jnp.zeros((2 * GATE_ROWS - N_SPLIT * N_HEADS, t), jnp.float32)
    stacked = _bf16(jnp.concatenate(parts + [pad], axis=0))
    cs = _dot(stacked, tri_ref[...])
    c = _tile_lanes(carry, t) + cs[0:N_HEADS]
    for p in range(1, N_SPLIT):
        c = c + cs[p * N_HEADS:(p + 1) * N_HEADS]
    total = jnp.sum(logf, axis=1, keepdims=True)
    return c, total


def _key_bias_columns(c, n_valid):
    t = c.shape[1]
    parts = _split_bf16(-c, N_SPLIT)
    if n_valid < t:
        lane = lax.broadcasted_iota(jnp.int32, c.shape, 1)
        parts[0] = jnp.where(lane < n_valid, parts[0], MASKED)
    pad = jnp.zeros((LANES - N_SPLIT * N_HEADS, t), jnp.float32)
    return _bf16(jnp.concatenate(parts + [pad], axis=0).T)


def _meta_inproj_kernel(h_ref, n1w_ref, wt_ref, wku_ref, bfg_ref, kw_ref, bd_ref, tri_ref,
                        k_ref, cp_ref, vt_ref, u_ref, c_ref):
    proj_t, ku = _norm_project(h_ref[...], n1w_ref, wt_ref, wku_ref)
    vt_ref[...] = _bf16(proj_t[D_ATTN:2 * D_ATTN])
    k_ref[...] = _bf16(_key_norm(ku[:, 0:D_ATTN], kw_ref, bd_ref))
    u_ref[...] = ku[0:N_META, D_ATTN:]
    zero = jnp.zeros((N_HEADS, LANES), jnp.float32)
    c, total = _gate_prefix(proj_t[2 * D_ATTN:2 * D_ATTN + N_HEADS], bfg_ref, tri_ref, zero, N_META)
    cp_ref[...] = _key_bias_columns(c, N_META)
    c_ref[...] = jnp.broadcast_to(total, (N_HEADS, LANES))


def _inproj_kernel(x_ref, um_ref, cm_ref, n1w_ref, wt_ref, wku_ref, bfg_ref, qw_ref, kw_ref, bd_ref,
                   tri_ref, wp_ref, ps_ref,
                   qt_ref, k_ref, cp_ref, vt_ref, po_ref,
                   ubuf_ref, carry_ref):
    t = x_ref.shape[0]

    @pl.when(pl.program_id(1) == 0)
    def _():
        ubuf_ref[0:N_META, :] = um_ref[...]
        carry_ref[...] = cm_ref[...]

    proj_t, ku = _norm_project(x_ref[...], n1w_ref, wt_ref, wku_ref)
    vt_ref[...] = _bf16(proj_t[D_ATTN:2 * D_ATTN])

    q3 = proj_t[0:D_ATTN].reshape(N_HEADS, HEAD_DIM, t)
    q3 = q3 * lax.rsqrt(jnp.mean(q3 * q3, axis=1, keepdims=True) + EPS)
    qt_ref[...] = _bf16(q3.reshape(D_ATTN, t) * _tile_lanes(qw_ref[...], t))

    k_ref[...] = _bf16(_key_norm(ku[:, 0:D_ATTN], kw_ref, bd_ref))

    carry = carry_ref[...]
    c, total = _gate_prefix(proj_t[2 * D_ATTN:2 * D_ATTN + N_HEADS], bfg_ref, tri_ref, carry, t)
    cp_ref[...] = _key_bias_columns(c, t)
    carry_ref[...] = carry + total

    ubuf_ref[N_META:, :] = ku[:, D_ATTN:]
    groups = []
    for g, w in enumerate(POOL_WINDOWS):
        lanes = slice(g * POOL_GROUP_DIM, (g + 1) * POOL_GROUP_DIM)
        cur = ubuf_ref[N_META:, lanes]
        acc = cur
        for j in range(1, w):
            acc = acc + ubuf_ref[N_META - j:N_META - j + t, lanes]
        groups.append(acc * (1.0 / w) - cur)
    pooled = _bf16(jnp.concatenate(groups, axis=1))
    ubuf_ref[0:N_META, :] = ubuf_ref[t:t + N_META, :]
    halves = []
    for c2 in range(D_POOL // MXU_TILE):
        halves.append(_dot(pooled[:, c2 * MXU_TILE:(c2 + 1) * MXU_TILE], wp_ref[c2]))
    po_ref[...] = _bf16(jnp.concatenate(halves, axis=1) * ps_ref[...])


def _attn_kernel(qt_ref, k_ref, cp_ref, vt_ref, km_ref, cpm_ref, vtm_ref, o_ref,
                 qa_ref, s_ref, cmax_ref, m_ref, acc_ref):
    pair = pl.program_id(1)
    qi = pl.program_id(2)
    bq = qt_ref.shape[1]
    bk = KV_BLOCK
    ones = jnp.ones((ONES_ROWS, bk), jnp.bfloat16)

    qt = qt_ref[...]
    zeros_q = jnp.zeros((HEAD_DIM, bq), qt.dtype)
    row = lax.broadcasted_iota(jnp.int32, (LANES, bq), 0)
    for hh in range(HEADS_PER_STEP):
        head = pair * HEADS_PER_STEP + hh
        pieces = [qt[r * HEAD_DIM:(r + 1) * HEAD_DIM] if r == hh else zeros_q
                  for r in range(HEADS_PER_STEP)]
        sel = ((row & (N_HEADS - 1)) == head) & (row < N_SPLIT * N_HEADS)
        qa_ref[hh] = jnp.concatenate(pieces + [jnp.where(sel, 1.0, 0.0).astype(qt.dtype)], axis=0)

    def v_aug(v_blk):
        return jnp.concatenate([v_blk, ones[:, 0:v_blk.shape[1]]], axis=0)

    km_aug = jnp.concatenate([km_ref[...], cpm_ref[...]], axis=1)
    for hh in range(HEADS_PER_STEP):
        s = _dot(km_aug, qa_ref[hh])
        m = jnp.max(s, axis=0, keepdims=True)
        m_ref[hh] = m
        acc_ref[hh] = _dot(v_aug(vtm_ref[hh * HEAD_DIM:(hh + 1) * HEAD_DIM, :]), _bf16(jnp.exp(s - m)))

    def scores(n, hh, diagonal):
        start = pl.multiple_of(n * bk, bk)
        k_aug = jnp.concatenate([k_ref[pl.ds(start, bk), :], cp_ref[pl.ds(start, bk), :]], axis=1)
        s = _dot(k_aug, qa_ref[hh])
        if diagonal:
            causal = (lax.broadcasted_iota(jnp.int32, (bk, bq), 0)
                      <= lax.broadcasted_iota(jnp.int32, (bk, bq), 1))
            s = jnp.where(causal, s, MASKED)
        s_ref[hh] = s
        cmax_ref[hh] = jnp.max(s, axis=0, keepdims=True)

    def update(n, hh):
        start = pl.multiple_of(n * bk, bk)
        m_old = m_ref[hh]
        m_new = jnp.maximum(m_old, cmax_ref[hh])
        p = _bf16(jnp.exp(s_ref[hh] - m_new))
        v_blk = vt_ref[hh * HEAD_DIM:(hh + 1) * HEAD_DIM, pl.ds(start, bk)]
        acc_ref[hh] = jnp.exp(m_old - m_new) * acc_ref[hh] + _dot(v_aug(v_blk), p)
        m_ref[hh] = m_new

    @pl.when(qi > 0)
    def _():
        scores(0, 0, False)

        def body(n, carry):
            scores(n, 1, False)
            update(n, 0)
            scores(n + 1, 0, False)
            update(n, 1)
            return carry

        lax.fori_loop(0, qi - 1, body, 0)
        scores(qi - 1, 1, False)
        update(qi - 1, 0)
        scores(qi, 0, True)
        update(qi - 1, 1)

    @pl.when(qi == 0)
    def _():
        scores(qi, 0, True)

    scores(qi, 1, True)
    update(qi, 0)
    update(qi, 1)

    outs = []
    for hh in range(HEADS_PER_STEP):
        acc = acc_ref[hh]
        outs.append(acc[0:HEAD_DIM] / acc[HEAD_DIM:HEAD_DIM + 1])
    o_ref[...] = _bf16(jnp.concatenate(outs, axis=0).T)


def _out_ffn_kernel(x_ref, a_ref, p_ref, wo_ref, n2w_ref, wg_ref, wu_ref, wd_ref, o_ref):
    mix = _dot(jnp.concatenate([a_ref[...], p_ref[...]], axis=1), wo_ref[...])
    h1 = x_ref[...] + mix
    n2 = _bf16(h1 * _rms_scale(h1) * n2w_ref[...])
    acc = h1
    lo = 0
    for width in FF_CHUNKS:
        g = _dot(n2, wg_ref[:, lo:lo + width])
        u = _dot(n2, wu_ref[:, lo:lo + width])
        acc = acc + _dot(_bf16(g * jax.nn.sigmoid(g) * u), wd_ref[lo:lo + width, :])
        lo += width
    o_ref[...] = acc


def _const_spec(shape, single_buffer=False):
    zeros = (0,) * len(shape)
    if single_buffer:
        return pl.BlockSpec(shape, lambda *_: zeros, pipeline_mode=pl.Buffered(1))
    return pl.BlockSpec(shape, lambda *_: zeros)


def _block_diag(blocks):
    n, r, c = blocks.shape
    out = jnp.zeros((n * r, n * c), blocks.dtype)
    for i in range(n):
        out = out.at[i * r:(i + 1) * r, i * c:(i + 1) * c].set(blocks[i])
    return out


def kernel(x, meta_tokens, norm1_w, w_in, b_fgate, q_norm_w, k_norm_w, w_pool, pool_scale, w_out,
           norm2_w, w_gate, w_up, w_down):
    batch, seq, d_model = x.shape
    assert w_in.shape[0] == 1, "one layer"
    assert seq % ROW_BLOCK == 0 and seq % Q_BLOCK == 0 and Q_BLOCK == KV_BLOCK
    d_ff = w_gate.shape[-1]
    assert sum(FF_CHUNKS) == d_ff
    f32, bf16 = jnp.float32, jnp.bfloat16

    w = w_in[0]
    w_q, w_k, w_v = (w[:, i * D_ATTN:(i + 1) * D_ATTN] for i in range(3))
    w_fg = w[:, 3 * D_ATTN:3 * D_ATTN + N_HEADS]
    w_u = w[:, 3 * D_ATTN + N_HEADS:]
    wt = jnp.concatenate([w_q.T, w_v.T, w_fg.T, jnp.zeros((GATE_ROWS - N_HEADS, d_model), f32)],
                         axis=0).astype(bf16)
    wku = jnp.concatenate([w_k, w_u], axis=1).astype(bf16)
    n1w = norm1_w[0].reshape(1, d_model)
    n2w = norm2_w[0].reshape(1, d_model)
    bfg = jnp.broadcast_to(b_fgate[0].reshape(N_HEADS, 1), (N_HEADS, LANES))
    qw = jnp.broadcast_to(jnp.tile(q_norm_w[0] * HEAD_DIM ** -0.5, N_HEADS).reshape(D_ATTN, 1),
                          (D_ATTN, LANES))
    kw = jnp.tile(k_norm_w[0], N_HEADS).reshape(1, D_ATTN)
    bd = _block_diag(jnp.ones((MXU_TILE // HEAD_DIM, HEAD_DIM, HEAD_DIM), bf16))
    wp = jnp.stack([_block_diag(w_pool[0, 2 * i:2 * i + 2]) for i in range(D_POOL // MXU_TILE)]
                   ).astype(bf16)
    ps = pool_scale[0].reshape(1, D_POOL)
    wo = w_out[0].astype(bf16)
    wg = w_gate[0].astype(bf16)
    wu = w_up[0].astype(bf16)
    wd = w_down[0].astype(bf16)

    def upper_tri(n):
        return (lax.broadcasted_iota(jnp.int32, (n, n), 0)
                <= lax.broadcasted_iota(jnp.int32, (n, n), 1)).astype(bf16)

    meta_blk = jnp.zeros((META_BLOCK, d_model), f32).at[0:N_META].set(meta_tokens)
    n_proj_t = 2 * D_ATTN + GATE_ROWS

    km, cpm, vtm, um, cm = pl.pallas_call(
        _meta_inproj_kernel,
        out_shape=(jax.ShapeDtypeStruct((META_BLOCK, D_ATTN), bf16),
                   jax.ShapeDtypeStruct((META_BLOCK, LANES), bf16),
                   jax.ShapeDtypeStruct((D_ATTN, META_BLOCK), bf16),
                   jax.ShapeDtypeStruct((N_META, D_POOL), f32),
                   jax.ShapeDtypeStruct((N_HEADS, LANES), f32)),
        compiler_params=pltpu.CompilerParams(vmem_limit_bytes=VMEM_LIMIT_BYTES),
        name="meta_inproj",
    )(meta_blk, n1w, wt, wku, bfg, kw, bd, upper_tri(META_BLOCK))

    n_row_blocks = seq // ROW_BLOCK
    tok = lambda b, j: (b, j, 0)
    tok_t = lambda b, j: (b, 0, j)
    qt, k, cp, vt, po = pl.pallas_call(
        _inproj_kernel,
        grid=(batch, n_row_blocks),
        in_specs=[pl.BlockSpec((None, ROW_BLOCK, d_model), tok),
                  _const_spec((N_META, D_POOL)), _const_spec((N_HEADS, LANES)),
                  _const_spec((1, d_model)), _const_spec((n_proj_t, d_model)),
                  _const_spec((d_model, D_ATTN + D_POOL)), _const_spec((N_HEADS, LANES)),
                  _const_spec((D_ATTN, LANES)), _const_spec((1, D_ATTN)),
                  _const_spec((MXU_TILE, MXU_TILE)), _const_spec((ROW_BLOCK, ROW_BLOCK)),
                  _const_spec((D_POOL // MXU_TILE, MXU_TILE, MXU_TILE)), _const_spec((1, D_POOL))],
        out_specs=(pl.BlockSpec((None, D_ATTN, ROW_BLOCK), tok_t),
                   pl.BlockSpec((None, ROW_BLOCK, D_ATTN), tok),
                   pl.BlockSpec((None, ROW_BLOCK, LANES), tok),
                   pl.BlockSpec((None, D_ATTN, ROW_BLOCK), tok_t),
                   pl.BlockSpec((None, ROW_BLOCK, D_POOL), tok)),
        out_shape=(jax.ShapeDtypeStruct((batch, D_ATTN, seq), bf16),
                   jax.ShapeDtypeStruct((batch, seq, D_ATTN), bf16),
                   jax.ShapeDtypeStruct((batch, seq, LANES), bf16),
                   jax.ShapeDtypeStruct((batch, D_ATTN, seq), bf16),
                   jax.ShapeDtypeStruct((batch, seq, D_POOL), bf16)),
        scratch_shapes=[pltpu.VMEM((N_META + ROW_BLOCK, D_POOL), f32),
                        pltpu.VMEM((N_HEADS, LANES), f32)],
        compiler_params=pltpu.CompilerParams(
            dimension_semantics=("arbitrary", "arbitrary"), vmem_limit_bytes=VMEM_LIMIT_BYTES),
        name="inproj",
    )(x, um, cm, n1w, wt, wku, bfg, qw, kw, bd, upper_tri(ROW_BLOCK), wp, ps)

    n_pairs = N_HEADS // HEADS_PER_STEP
    a = pl.pallas_call(
        _attn_kernel,
        grid=(batch, n_pairs, seq // Q_BLOCK),
        in_specs=[pl.BlockSpec((None, LANES, Q_BLOCK), lambda b, p, i: (b, p, i)),
                  pl.BlockSpec((None, seq, LANES), lambda b, p, i: (b, 0, p)),
                  pl.BlockSpec((None, seq, LANES), lambda b, p, i: (b, 0, 0)),
                  pl.BlockSpec((None, LANES, seq), lambda b, p, i: (b, p, 0)),
                  pl.BlockSpec((META_BLOCK, LANES), lambda b, p, i: (0, p)),
                  pl.BlockSpec((META_BLOCK, LANES), lambda b, p, i: (0, 0)),
                  pl.BlockSpec((LANES, META_BLOCK), lambda b, p, i: (p, 0))],
        out_specs=pl.BlockSpec((None, Q_BLOCK, LANES), lambda b, p, i: (b, i, p)),
        out_shape=jax.ShapeDtypeStruct((batch, seq, D_ATTN), bf16),
        scratch_shapes=[pltpu.VMEM((HEADS_PER_STEP, 2 * LANES, Q_BLOCK), bf16),
                        pltpu.VMEM((HEADS_PER_STEP, KV_BLOCK, Q_BLOCK), f32),
                        pltpu.VMEM((HEADS_PER_STEP, 1, Q_BLOCK), f32),
                        pltpu.VMEM((HEADS_PER_STEP, 1, Q_BLOCK), f32),
                        pltpu.VMEM((HEADS_PER_STEP, HEAD_DIM + ONES_ROWS, Q_BLOCK), f32)],
        compiler_params=pltpu.CompilerParams(
            dimension_semantics=("parallel", "parallel", "arbitrary"),
            vmem_limit_bytes=VMEM_LIMIT_BYTES),
        name="fox_attn",
    )(qt, k, cp, vt, km, cpm, vtm)

    rows = batch * seq
    row = lambda r: (r, 0)
    out = pl.pallas_call(
        _out_ffn_kernel,
        grid=(rows // ROW_BLOCK,),
        in_specs=[pl.BlockSpec((ROW_BLOCK, d_model), row),
                  pl.BlockSpec((ROW_BLOCK, D_ATTN), row),
                  pl.BlockSpec((ROW_BLOCK, D_POOL), row),
                  _const_spec((D_ATTN + D_POOL, d_model), True), _const_spec((1, d_model)),
                  _const_spec((d_model, d_ff), True), _const_spec((d_model, d_ff), True),
                  _const_spec((d_ff, d_model), True)],
        out_specs=pl.BlockSpec((ROW_BLOCK, d_model), row),
        out_shape=jax.ShapeDtypeStruct((rows, d_model), f32),
        compiler_params=pltpu.CompilerParams(
            dimension_semantics=("parallel",), vmem_limit_bytes=VMEM_LIMIT_BYTES),
        name="out_ffn",
    )(x.reshape(rows, d_model), a.reshape(rows, D_ATTN), po.reshape(rows, D_POOL),
      wo, n2w, wg, wu, wd)
    return out.reshape(batch, seq, d_model)
```

```python
import functools

import jax
import jax.numpy as jnp
from jax import lax
from jax.experimental import pallas as pl
from jax.experimental.pallas import tpu as pltpu

N_META = 16
N_HEADS = 8
HEAD_DIM = 64
D_ATTN = N_HEADS * HEAD_DIM
POOL_WINDOWS = (2, 4, 8, 16)
POOL_GROUP_DIM = 128
D_POOL = len(POOL_WINDOWS) * POOL_GROUP_DIM
EPS = 1e-6

LANES = 128
MXU_TILE = 256
HEADS_PER_STEP = LANES // HEAD_DIM
N_SPLIT = 3
GATE_ROWS = 16
ONES_ROWS = 16
MASKED = -1e30
LOG2E = 1.4426950408889634

META_BLOCK = 128
ROW_BLOCK = 512
Q_BLOCK = 512
KV_BLOCK = 512
FF_CHUNKS = (1024, 1024, 768)
VMEM_LIMIT_BYTES = 56 * 1024 * 1024


def _f32(x):
    return x.astype(jnp.float32)


def _bf16(x):
    return x.astype(jnp.bfloat16)


def _dot(a, b):
    return jnp.dot(a, b, preferred_element_type=jnp.float32)


def _dot_nt(a, b):
    return lax.dot_general(a, b, (((1,), (1,)), ((), ())), preferred_element_type=jnp.float32)


def _split_bf16(x, n):
    parts = []
    r = x
    for _ in range(n):
        p = _f32(_bf16(r))
        parts.append(p)
        r = r - p
    return parts


def _rms_scale(x):
    return lax.rsqrt(jnp.mean(x * x, axis=-1, keepdims=True) + EPS)


def _tile_lanes(x, n):
    return jnp.concatenate([x] * (n // x.shape[1]), axis=1) if n != x.shape[1] else x


def _norm_project(h, n1w_ref, wt_ref, wku_ref):
    n1 = _bf16(h * _rms_scale(h) * n1w_ref[...])
    proj_t = _dot_nt(wt_ref[...], n1)
    ku = _dot(n1, wku_ref[...])
    return proj_t, ku


def _key_norm(k, kw_ref, bd_ref):
    hi, lo = (_bf16(p) for p in _split_bf16(k * k, 2))
    bd = bd_ref[...]
    halves = []
    for c in range(D_ATTN // MXU_TILE):
        sl = slice(c * MXU_TILE, (c + 1) * MXU_TILE)
        halves.append(_dot(hi[:, sl], bd) + _dot(lo[:, sl], bd))
    ss = jnp.concatenate(halves, axis=1)
    return k * lax.rsqrt(ss * (1.0 / HEAD_DIM) + EPS) * kw_ref[...]


def _gate_prefix(fg_t, bfg_ref, tri_ref, carry, n_valid):
    t = fg_t.shape[1]
    logf = jax.nn.log_sigmoid(fg_t + _tile_lanes(bfg_ref[...], t))
    if n_valid < t:
        lane = lax.broadcasted_iota(jnp.int32, logf.shape, 1)
        logf = jnp.where(lane < n_valid, logf, 0.0)
    parts = _split_bf16(logf, N_SPLIT)
    pad = jnp.zeros((2 * GATE_ROWS - N_SPLIT * N_HEADS, t), jnp.float32)
    stacked = _bf16(jnp.concatenate(parts + [pad], axis=0))
    cs = _dot(stacked, tri_ref[...])
    c = _tile_lanes(carry, t) + cs[0:N_HEADS]
    for p in range(1, N_SPLIT):
        c = c + cs[p * N_HEADS:(p + 1) * N_HEADS]
    total = jnp.sum(logf, axis=1, keepdims=True)
    return c, total


def _key_bias_columns(c, n_valid):
    t = c.shape[1]
    parts = _split_bf16(c * -LOG2E, N_SPLIT)
    if n_valid < t:
        lane = lax.broadcasted_iota(jnp.int32, c.shape, 1)
        parts[0] = jnp.where(lane < n_valid, parts[0], MASKED)
    pad = jnp.zeros((LANES - N_SPLIT * N_HEADS, t), jnp.float32)
    return _bf16(jnp.concatenate(parts + [pad], axis=0).T)


def _meta_inproj_kernel(h_ref, n1w_ref, wt_ref, wku_ref, bfg_ref, kw_ref, bd_ref, tri_ref,
                        k_ref, cp_ref, vt_ref, u_ref, c_ref):
    proj_t, ku = _norm_project(h_ref[...], n1w_ref, wt_ref, wku_ref)
    vt_ref[...] = _bf16(proj_t[D_ATTN:2 * D_ATTN])
    k_ref[...] = _bf16(_key_norm(ku[:, 0:D_ATTN], kw_ref, bd_ref))
    u_ref[...] = ku[0:N_META, D_ATTN:]
    zero = jnp.zeros((N_HEADS, LANES), jnp.float32)
    c, total = _gate_prefix(proj_t[2 * D_ATTN:2 * D_ATTN + N_HEADS], bfg_ref, tri_ref, zero, N_META)
    cp_ref[...] = _key_bias_columns(c, N_META)
    c_ref[...] = jnp.broadcast_to(total, (N_HEADS, LANES))


def _inproj_kernel(x_ref, um_ref, cm_ref, n1w_ref, wt_ref, wku_ref, bfg_ref, qw_ref, kw_ref, bd_ref,
                   tri_ref, wp_ref, ps_ref,
                   qt_ref, k_ref, cp_ref, vt_ref, po_ref,
                   ubuf_ref, carry_ref):
    t = x_ref.shape[0]

    @pl.when(pl.program_id(1) == 0)
    def _():
        ubuf_ref[0:N_META, :] = um_ref[...]
        carry_ref[...] = cm_ref[...]

    proj_t, ku = _norm_project(x_ref[...], n1w_ref, wt_ref, wku_ref)
    vt_ref[...] = _bf16(proj_t[D_ATTN:2 * D_ATTN])

    q3 = proj_t[0:D_ATTN].reshape(N_HEADS, HEAD_DIM, t)
    q3 = q3 * lax.rsqrt(jnp.mean(q3 * q3, axis=1, keepdims=True) + EPS)
    qt_ref[...] = _bf16(q3.reshape(D_ATTN, t) * _tile_lanes(qw_ref[...], t))

    k_ref[...] = _bf16(_key_norm(ku[:, 0:D_ATTN], kw_ref, bd_ref))

    carry = carry_ref[...]
    c, total = _gate_prefix(proj_t[2 * D_ATTN:2 * D_ATTN + N_HEADS], bfg_ref, tri_ref, carry, t)
    cp_ref[...] = _key_bias_columns(c, t)
    carry_ref[...] = carry + total

    ubuf_ref[N_META:, :] = ku[:, D_ATTN:]
    groups = []
    for g, w in enumerate(POOL_WINDOWS):
        lanes = slice(g * POOL_GROUP_DIM, (g + 1) * POOL_GROUP_DIM)
        cur = ubuf_ref[N_META:, lanes]
        acc = cur
        for j in range(1, w):
            acc = acc + ubuf_ref[N_META - j:N_META - j + t, lanes]
        groups.append(acc * (1.0 / w) - cur)
    pooled = _bf16(jnp.concatenate(groups, axis=1))
    ubuf_ref[0:N_META, :] = ubuf_ref[t:t + N_META, :]
    halves = []
    for c2 in range(D_POOL // MXU_TILE):
        halves.append(_dot(pooled[:, c2 * MXU_TILE:(c2 + 1) * MXU_TILE], wp_ref[c2]))
    po_ref[...] = _bf16(jnp.concatenate(halves, axis=1) * ps_ref[...])


def _attn_kernel(qt_ref, k_ref, cp_ref, vt_ref, km_ref, cpm_ref, vtm_ref, o_ref,
                 qa_ref, s_ref, cmax_ref, m_ref, acc_ref):
    pair = pl.program_id(1)
    qi = pl.program_id(2)
    bq = qt_ref.shape[1]
    bk = KV_BLOCK
    heads = range(HEADS_PER_STEP)
    ones = jnp.ones((ONES_ROWS, bk), jnp.bfloat16)

    qt = qt_ref[...]
    zeros_q = jnp.zeros((HEAD_DIM, bq), qt.dtype)
    row = lax.broadcasted_iota(jnp.int32, (LANES, bq), 0)
    for hh in heads:
        head = pair * HEADS_PER_STEP + hh
        pieces = [qt[r * HEAD_DIM:(r + 1) * HEAD_DIM] if r == hh else zeros_q for r in heads]
        sel = ((row & (N_HEADS - 1)) == head) & (row < N_SPLIT * N_HEADS)
        qa_ref[hh] = jnp.concatenate(pieces + [jnp.where(sel, 1.0, 0.0).astype(qt.dtype)], axis=0)
        m_ref[hh] = jnp.full(m_ref.shape[1:], MASKED, jnp.float32)
        acc_ref[hh] = jnp.zeros(acc_ref.shape[1:], jnp.float32)

    def scores(k_aug, slot, hh, diagonal=False):
        rows = k_aug.shape[0]
        s = _dot(k_aug, qa_ref[hh])
        if diagonal:
            causal = (lax.broadcasted_iota(jnp.int32, (rows, bq), 0)
                      <= lax.broadcasted_iota(jnp.int32, (rows, bq), 1))
            s = jnp.where(causal, s, MASKED)
        s_ref[slot, hh, 0:rows, :] = s
        cmax_ref[slot, hh] = jnp.max(s, axis=0, keepdims=True)

    def update(v_blk, slot, hh):
        rows = v_blk.shape[1]
        m_old = m_ref[hh]
        m_new = jnp.maximum(m_old, cmax_ref[slot, hh])
        p = _bf16(jnp.exp2(s_ref[slot, hh, 0:rows, :] - m_new))
        v_aug = jnp.concatenate([v_blk, ones[:, 0:rows]], axis=0)
        acc_ref[hh] = jnp.exp2(m_old - m_new) * acc_ref[hh] + _dot(v_aug, p)
        m_ref[hh] = m_new

    def k_tile(n):
        start = pl.multiple_of(n * bk, bk)
        return jnp.concatenate([k_ref[pl.ds(start, bk), :], cp_ref[pl.ds(start, bk), :]], axis=1)

    def v_tile(n):
        start = pl.multiple_of(n * bk, bk)
        return lambda hh: vt_ref[hh * HEAD_DIM:(hh + 1) * HEAD_DIM, pl.ds(start, bk)]

    def k_meta():
        return jnp.concatenate([km_ref[...], cpm_ref[...]], axis=1)

    def v_meta(hh):
        return vtm_ref[hh * HEAD_DIM:(hh + 1) * HEAD_DIM, :]

    def stage(k_aug, s_slot, v_blk, u_slot, diagonal=False):
        for hh in heads:
            if k_aug is not None:
                scores(k_aug, s_slot, hh, diagonal)
            if v_blk is not None:
                update(v_blk(hh), u_slot, hh)

    def tile_at(j):
        return jnp.where(j == 0, qi, j - 1)

    stage(k_tile(qi), 0, None, None, diagonal=True)

    def two_stages(i, carry):
        j = 2 * i
        stage(k_tile(j), 1, v_tile(tile_at(j)), 0)
        stage(k_tile(j + 1), 0, v_tile(j), 1)
        return carry

    lax.fori_loop(0, lax.shift_right_logical(qi, 1), two_stages, 0)

    @pl.when((qi & 1) == 0)
    def _():
        stage(k_meta(), 1, v_tile(tile_at(qi)), 0)
        stage(None, None, v_meta, 1)

    @pl.when((qi & 1) == 1)
    def _():
        stage(k_tile(qi - 1), 1, v_tile(tile_at(qi - 1)), 0)
        stage(k_meta(), 0, v_tile(qi - 1), 1)
        stage(None, None, v_meta, 0)

    outs = []
    for hh in range(HEADS_PER_STEP):
        acc = acc_ref[hh]
        outs.append(acc[0:HEAD_DIM] / acc[HEAD_DIM:HEAD_DIM + 1])
    o_ref[...] = _bf16(jnp.concatenate(outs, axis=0).T)


def _out_ffn_kernel(x_ref, a_ref, p_ref, wo_ref, n2w_ref, wg_ref, wu_ref, wd_ref, o_ref):
    mix = _dot(jnp.concatenate([a_ref[...], p_ref[...]], axis=1), wo_ref[...])
    h1 = x_ref[...] + mix
    n2 = _bf16(h1 * _rms_scale(h1) * n2w_ref[...])
    acc = h1
    lo = 0
    for width in FF_CHUNKS:
        g = _dot(n2, wg_ref[:, lo:lo + width])
        u = _dot(n2, wu_ref[:, lo:lo + width])
        acc = acc + _dot(_bf16(g * jax.nn.sigmoid(g) * u), wd_ref[lo:lo + width, :])
        lo += width
    o_ref[...] = acc


def _const_spec(shape, single_buffer=False):
    zeros = (0,) * len(shape)
    if single_buffer:
        return pl.BlockSpec(shape, lambda *_: zeros, pipeline_mode=pl.Buffered(1))
    return pl.BlockSpec(shape, lambda *_: zeros)


def _block_diag(blocks):
    n, r, c = blocks.shape
    out = jnp.zeros((n * r, n * c), blocks.dtype)
    for i in range(n):
        out = out.at[i * r:(i + 1) * r, i * c:(i + 1) * c].set(blocks[i])
    return out


def kernel(x, meta_tokens, norm1_w, w_in, b_fgate, q_norm_w, k_norm_w, w_pool, pool_scale, w_out,
           norm2_w, w_gate, w_up, w_down):
    batch, seq, d_model = x.shape
    assert w_in.shape[0] == 1, "one layer"
    assert seq % ROW_BLOCK == 0 and seq % Q_BLOCK == 0 and Q_BLOCK == KV_BLOCK
    d_ff = w_gate.shape[-1]
    assert sum(FF_CHUNKS) == d_ff
    f32, bf16 = jnp.float32, jnp.bfloat16

    w = w_in[0]
    w_q, w_k, w_v = (w[:, i * D_ATTN:(i + 1) * D_ATTN] for i in range(3))
    w_fg = w[:, 3 * D_ATTN:3 * D_ATTN + N_HEADS]
    w_u = w[:, 3 * D_ATTN + N_HEADS:]
    wt = jnp.concatenate([w_q.T, w_v.T, w_fg.T, jnp.zeros((GATE_ROWS - N_HEADS, d_model), f32)],
                         axis=0).astype(bf16)
    wku = jnp.concatenate([w_k, w_u], axis=1).astype(bf16)
    n1w = norm1_w[0].reshape(1, d_model)
    n2w = norm2_w[0].reshape(1, d_model)
    bfg = jnp.broadcast_to(b_fgate[0].reshape(N_HEADS, 1), (N_HEADS, LANES))
    qw = jnp.broadcast_to(jnp.tile(q_norm_w[0] * (HEAD_DIM ** -0.5 * LOG2E), N_HEADS).reshape(D_ATTN, 1),
                          (D_ATTN, LANES))
    kw = jnp.tile(k_norm_w[0], N_HEADS).reshape(1, D_ATTN)
    bd = _block_diag(jnp.ones((MXU_TILE // HEAD_DIM, HEAD_DIM, HEAD_DIM), bf16))
    wp = jnp.stack([_block_diag(w_pool[0, 2 * i:2 * i + 2]) for i in range(D_POOL // MXU_TILE)]
                   ).astype(bf16)
    ps = pool_scale[0].reshape(1, D_POOL)
    wo = w_out[0].astype(bf16)
    wg = w_gate[0].astype(bf16)
    wu = w_up[0].astype(bf16)
    wd = w_down[0].astype(bf16)

    def upper_tri(n):
        return (lax.broadcasted_iota(jnp.int32, (n, n), 0)
                <= lax.broadcasted_iota(jnp.int32, (n, n), 1)).astype(bf16)

    meta_blk = jnp.zeros((META_BLOCK, d_model), f32).at[0:N_META].set(meta_tokens)
    n_proj_t = 2 * D_ATTN + GATE_ROWS

    km, cpm, vtm, um, cm = pl.pallas_call(
        _meta_inproj_kernel,
        out_shape=(jax.ShapeDtypeStruct((META_BLOCK, D_ATTN), bf16),
                   jax.ShapeDtypeStruct((META_BLOCK, LANES), bf16),
                   jax.ShapeDtypeStruct((D_ATTN, META_BLOCK), bf16),
                   jax.ShapeDtypeStruct((N_META, D_POOL), f32),
                   jax.ShapeDtypeStruct((N_HEADS, LANES), f32)),
        compiler_params=pltpu.CompilerParams(vmem_limit_bytes=VMEM_LIMIT_BYTES),
        name="meta_inproj",
    )(meta_blk, n1w, wt, wku, bfg, kw, bd, upper_tri(META_BLOCK))

    n_row_blocks = seq // ROW_BLOCK
    tok = lambda b, j: (b, j, 0)
    tok_t = lambda b, j: (b, 0, j)
    qt, k, cp, vt, po = pl.pallas_call(
        _inproj_kernel,
        grid=(batch, n_row_blocks),
        in_specs=[pl.BlockSpec((None, ROW_BLOCK, d_model), tok),
                  _const_spec((N_META, D_POOL)), _const_spec((N_HEADS, LANES)),
                  _const_spec((1, d_model)), _const_spec((n_proj_t, d_model)),
                  _const_spec((d_model, D_ATTN + D_POOL)), _const_spec((N_HEADS, LANES)),
                  _const_spec((D_ATTN, LANES)), _const_spec((1, D_ATTN)),
                  _const_spec((MXU_TILE, MXU_TILE)), _const_spec((ROW_BLOCK, ROW_BLOCK)),
                  _const_spec((D_POOL // MXU_TILE, MXU_TILE, MXU_TILE)), _const_spec((1, D_POOL))],
        out_specs=(pl.BlockSpec((None, D_ATTN, ROW_BLOCK), tok_t),
                   pl.BlockSpec((None, ROW_BLOCK, D_ATTN), tok),
                   pl.BlockSpec((None, ROW_BLOCK, LANES), tok),
                   pl.BlockSpec((None, D_ATTN, ROW_BLOCK), tok_t),
                   pl.BlockSpec((None, ROW_BLOCK, D_POOL), tok)),
        out_shape=(jax.ShapeDtypeStruct((batch, D_ATTN, seq), bf16),
                   jax.ShapeDtypeStruct((batch, seq, D_ATTN), bf16),
                   jax.ShapeDtypeStruct((batch, seq, LANES), bf16),
                   jax.ShapeDtypeStruct((batch, D_ATTN, seq), bf16),
                   jax.ShapeDtypeStruct((batch, seq, D_POOL), bf16)),
        scratch_shapes=[pltpu.VMEM((N_META + ROW_BLOCK, D_POOL), f32),
                        pltpu.VMEM((N_HEADS, LANES), f32)],
        compiler_params=pltpu.CompilerParams(
            dimension_semantics=("arbitrary", "arbitrary"), vmem_limit_bytes=VMEM_LIMIT_BYTES),
        name="inproj",
    )(x, um, cm, n1w, wt, wku, bfg, qw, kw, bd, upper_tri(ROW_BLOCK), wp, ps)

    n_pairs = N_HEADS // HEADS_PER_STEP
    a = pl.pallas_call(
        _attn_kernel,
        grid=(batch, n_pairs, seq // Q_BLOCK),
        in_specs=[pl.BlockSpec((None, LANES, Q_BLOCK), lambda b, p, i: (b, p, i)),
                  pl.BlockSpec((None, seq, LANES), lambda b, p, i: (b, 0, p)),
                  pl.BlockSpec((None, seq, LANES), lambda b, p, i: (b, 0, 0)),
                  pl.BlockSpec((None, LANES, seq), lambda b, p, i: (b, p, 0)),
                  pl.BlockSpec((META_BLOCK, LANES), lambda b, p, i: (0, p)),
                  pl.BlockSpec((META_BLOCK, LANES), lambda b, p, i: (0, 0)),
                  pl.BlockSpec((LANES, META_BLOCK), lambda b, p, i: (p, 0))],
        out_specs=pl.BlockSpec((None, Q_BLOCK, LANES), lambda b, p, i: (b, i, p)),
        out_shape=jax.ShapeDtypeStruct((batch, seq, D_ATTN), bf16),
        scratch_shapes=[pltpu.VMEM((HEADS_PER_STEP, 2 * LANES, Q_BLOCK), bf16),
                        pltpu.VMEM((2, HEADS_PER_STEP, KV_BLOCK, Q_BLOCK), f32),
                        pltpu.VMEM((2, HEADS_PER_STEP, 1, Q_BLOCK), f32),
                        pltpu.VMEM((HEADS_PER_STEP, 1, Q_BLOCK), f32),
                        pltpu.VMEM((HEADS_PER_STEP, HEAD_DIM + ONES_ROWS, Q_BLOCK), f32)],
        compiler_params=pltpu.CompilerParams(
            dimension_semantics=("parallel", "parallel", "arbitrary"),
            vmem_limit_bytes=VMEM_LIMIT_BYTES),
        name="fox_attn",
    )(qt, k, cp, vt, km, cpm, vtm)

    rows = batch * seq
    row = lambda r: (r, 0)
    out = pl.pallas_call(
        _out_ffn_kernel,
        grid=(rows // ROW_BLOCK,),
        in_specs=[pl.BlockSpec((ROW_BLOCK, d_model), row),
                  pl.BlockSpec((ROW_BLOCK, D_ATTN), row),
                  pl.BlockSpec((ROW_BLOCK, D_POOL), row),
                  _const_spec((D_ATTN + D_POOL, d_model), True), _const_spec((1, d_model)),
                  _const_spec((d_model, d_ff), True), _const_spec((d_model, d_ff), True),
                  _const_spec((d_ff, d_model), True)],
        out_specs=pl.BlockSpec((ROW_BLOCK, d_model), row),
        out_shape=jax.ShapeDtypeStruct((rows, d_model), f32),
        compiler_params=pltpu.CompilerParams(
            dimension_semantics=("parallel",), vmem_limit_bytes=VMEM_LIMIT_BYTES),
        name="out_ffn",
    )(x.reshape(rows, d_model), a.reshape(rows, D_ATTN), po.reshape(rows, D_POOL),
      wo, n2w, wg, wu, wd)
    return out.reshape(batch, seq, d_model)
```

```python
import functools

import jax
import jax.numpy as jnp
from jax import lax
from jax.experimental import pallas as pl
from jax.experimental.pallas import tpu as pltpu

N_META = 16
N_HEADS = 8
HEAD_DIM = 64
D_ATTN = N_HEADS * HEAD_DIM
POOL_WINDOWS = (2, 4, 8, 16)
POOL_GROUP_DIM = 128
D_POOL = len(POOL_WINDOWS) * POOL_GROUP_DIM
EPS = 1e-6

LANES = 128
MXU_TILE = 256
HEADS_PER_STEP = LANES // HEAD_DIM
N_SPLIT = 3
GATE_ROWS = 16
ONES_ROWS = 16
MASKED = -1e30
LOG2E = 1.4426950408889634

META_BLOCK = 128
INPROJ_BLOCK = 1024
INPROJ_SUB = 256
ROW_BLOCK = 512
Q_BLOCK = 512
KV_BLOCK = 512
Q_CHUNK = Q_BLOCK
FF_CHUNKS = (1024, 1024, 768)
VMEM_LIMIT_BYTES = 56 * 1024 * 1024


def _f32(x):
    return x.astype(jnp.float32)


def _bf16(x):
    return x.astype(jnp.bfloat16)


def _dot(a, b):
    return jnp.dot(a, b, preferred_element_type=jnp.float32)


def _dot_nt(a, b):
    return lax.dot_general(a, b, (((1,), (1,)), ((), ())), preferred_element_type=jnp.float32)


def _split_bf16(x, n):
    parts = []
    r = x
    for _ in range(n):
        p = _f32(_bf16(r))
        parts.append(p)
        r = r - p
    return parts


def _rms_scale(x):
    return lax.rsqrt(jnp.mean(x * x, axis=-1, keepdims=True) + EPS)


def _tile_lanes(x, n):
    return jnp.concatenate([x] * (n // x.shape[1]), axis=1) if n != x.shape[1] else x


def _norm_project(h, n1w_ref, wt_ref, wku_ref):
    n1 = _bf16(h * _rms_scale(h) * n1w_ref[...])
    proj_t = _dot_nt(wt_ref[...], n1)
    ku = _dot(n1, wku_ref[...])
    return proj_t, ku


def _key_norm(k, kw_ref, bd_ref):
    hi, lo = (_bf16(p) for p in _split_bf16(k * k, 2))
    bd = bd_ref[...]
    halves = []
    for c in range(D_ATTN // MXU_TILE):
        sl = slice(c * MXU_TILE, (c + 1) * MXU_TILE)
        halves.append(_dot(hi[:, sl], bd) + _dot(lo[:, sl], bd))
    ss = jnp.concatenate(halves, axis=1)
    return k * lax.rsqrt(ss * (1.0 / HEAD_DIM) + EPS) * kw_ref[...]


def _gate_prefix(fg_t, bfg_ref, tri_ref, carry, n_valid):
    t = fg_t.shape[1]
    logf = jax.nn.log_sigmoid(fg_t + _tile_lanes(bfg_ref[...], t))
    if n_valid < t:
        lane = lax.broadcasted_iota(jnp.int32, logf.shape, 1)
        logf = jnp.where(lane < n_valid, logf, 0.0)
    parts = _split_bf16(logf, N_SPLIT)
    pad = jnp.zeros((2 * GATE_ROWS - N_SPLIT * N_HEADS, t), jnp.float32)
    stacked = _bf16(jnp.concatenate(parts + [pad], axis=0))
    cs = _dot(stacked, tri_ref[...])
    c = _tile_lanes(carry, t) + cs[0:N_HEADS]
    for p in range(1, N_SPLIT):
        c = c + cs[p * N_HEADS:(p + 1) * N_HEADS]
    total = jnp.sum(logf, axis=1, keepdims=True)
    return c, total


def _key_bias_columns(c, n_valid):
    t = c.shape[1]
    parts = _split_bf16(c * -LOG2E, N_SPLIT)
    if n_valid < t:
        lane = lax.broadcasted_iota(jnp.int32, c.shape, 1)
        parts[0] = jnp.where(lane < n_valid, parts[0], MASKED)
    pad = jnp.zeros((LANES - N_SPLIT * N_HEADS, t), jnp.float32)
    return _bf16(jnp.concatenate(parts + [pad], axis=0).T)


def _meta_inproj_kernel(h_ref, n1w_ref, wt_ref, wku_ref, bfg_ref, kw_ref, bd_ref, tri_ref,
                        k_ref, cp_ref, vt_ref, u_ref, c_ref):
    proj_t, ku = _norm_project(h_ref[...], n1w_ref, wt_ref, wku_ref)
    vt_ref[...] = _bf16(proj_t[D_ATTN:2 * D_ATTN])
    k_ref[...] = _bf16(_key_norm(ku[:, 0:D_ATTN], kw_ref, bd_ref))
    u_ref[...] = ku[0:N_META, D_ATTN:]
    zero = jnp.zeros((N_HEADS, LANES), jnp.float32)
    c, total = _gate_prefix(proj_t[2 * D_ATTN:2 * D_ATTN + N_HEADS], bfg_ref, tri_ref, zero, N_META)
    cp_ref[...] = _key_bias_columns(c, N_META)
    c_ref[...] = jnp.broadcast_to(total, (N_HEADS, LANES))


def _inproj_kernel(x_ref, um_ref, cm_ref, n1w_ref, wt_ref, wku_ref, bfg_ref, qw_ref, kw_ref, bd_ref,
                   tri_ref, wp_ref, ps_ref,
                   qt_ref, k_ref, cp_ref, vt_ref, po_ref,
                   ubuf_ref, carry_ref):
    t = INPROJ_SUB

    @pl.when(pl.program_id(1) == 0)
    def _():
        ubuf_ref[0:N_META, :] = um_ref[...]
        carry_ref[...] = cm_ref[...]

    carry = carry_ref[...]
    n_sub = x_ref.shape[0] // t

    def project(r):
        return _norm_project(x_ref[r * t:(r + 1) * t, :], n1w_ref, wt_ref, wku_ref)

    projected = project(0)
    for r in range(n_sub):
        rows = slice(r * t, (r + 1) * t)
        proj_t, ku = projected
        if r + 1 < n_sub:
            projected = project(r + 1)
        vt_ref[:, rows] = _bf16(proj_t[D_ATTN:2 * D_ATTN])

        q3 = proj_t[0:D_ATTN].reshape(N_HEADS, HEAD_DIM, t)
        q3 = q3 * lax.rsqrt(jnp.mean(q3 * q3, axis=1, keepdims=True) + EPS)
        qt_ref[:, rows] = _bf16(q3.reshape(D_ATTN, t) * _tile_lanes(qw_ref[...], t))

        k_ref[rows, :] = _bf16(_key_norm(ku[:, 0:D_ATTN], kw_ref, bd_ref))

        c, total = _gate_prefix(proj_t[2 * D_ATTN:2 * D_ATTN + N_HEADS], bfg_ref, tri_ref, carry, t)
        cp_ref[rows, :] = _key_bias_columns(c, t)
        carry = carry + total

        base = N_META + r * t
        ubuf_ref[base:base + t, :] = ku[:, D_ATTN:]
        groups = []
        for g, w in enumerate(POOL_WINDOWS):
            lanes = slice(g * POOL_GROUP_DIM, (g + 1) * POOL_GROUP_DIM)
            cur = ubuf_ref[base:base + t, lanes]
            acc = cur
            for j in range(1, w):
                acc = acc + ubuf_ref[base - j:base - j + t, lanes]
            groups.append(acc * (1.0 / w) - cur)
        pooled = _bf16(jnp.concatenate(groups, axis=1))
        halves = []
        for c2 in range(D_POOL // MXU_TILE):
            halves.append(_dot(pooled[:, c2 * MXU_TILE:(c2 + 1) * MXU_TILE], wp_ref[c2]))
        po_ref[rows, :] = _bf16(jnp.concatenate(halves, axis=1) * ps_ref[...])
    carry_ref[...] = carry
    ubuf_ref[0:N_META, :] = ubuf_ref[x_ref.shape[0]:x_ref.shape[0] + N_META, :]


def _attn_kernel(qt_ref, k_ref, cp_ref, vt_ref, km_ref, cpm_ref, vtm_ref, o_ref,
                 qa_ref, s_ref, cmax_ref, m_ref, acc_ref):
    pair = pl.program_id(1)
    qi = pl.program_id(2)
    n_q = pl.num_programs(2)
    bq = o_ref.shape[0]
    bk = KV_BLOCK
    heads = range(HEADS_PER_STEP)
    ones = jnp.ones((ONES_ROWS, bk), jnp.bfloat16)
    meta_slot = 2

    def build_q_aug(blk):
        qt = qt_ref[:, pl.ds(pl.multiple_of(blk * bq, bq), bq)]
        zeros_q = jnp.zeros((HEAD_DIM, bq), qt.dtype)
        row = lax.broadcasted_iota(jnp.int32, (LANES, bq), 0)
        for hh in heads:
            head = pair * HEADS_PER_STEP + hh
            pieces = [qt[r * HEAD_DIM:(r + 1) * HEAD_DIM] if r == hh else zeros_q for r in heads]
            sel = ((row & (N_HEADS - 1)) == head) & (row < N_SPLIT * N_HEADS)
            qa_ref[hh] = jnp.concatenate(pieces + [jnp.where(sel, 1.0, 0.0).astype(qt.dtype)], axis=0)

    def scores(k_aug, slot, hh, cols, diagonal=False):
        rows = k_aug.shape[0]
        s = _dot(k_aug, qa_ref[hh, :, cols])
        if diagonal:
            causal = (lax.broadcasted_iota(jnp.int32, s.shape, 0)
                      <= lax.broadcasted_iota(jnp.int32, s.shape, 1) + cols.start)
            s = jnp.where(causal, s, MASKED)
        s_ref[slot, hh, 0:rows, cols] = s
        cmax_ref[slot, hh, :, cols] = jnp.max(s, axis=0, keepdims=True)

    def update(v_blk, slot, hh, cols):
        rows = v_blk.shape[1]
        m_old = m_ref[hh, :, cols]
        m_new = jnp.maximum(m_old, cmax_ref[slot, hh, :, cols])
        p = _bf16(jnp.exp2(s_ref[slot, hh, 0:rows, cols] - m_new))
        v_aug = jnp.concatenate([v_blk, ones[:, 0:rows]], axis=0)
        acc_ref[hh, :, cols] = jnp.exp2(m_old - m_new) * acc_ref[hh, :, cols] + _dot(v_aug, p)
        m_ref[hh, :, cols] = m_new

    def k_tile(n):
        start = pl.multiple_of(n * bk, bk)
        return jnp.concatenate([k_ref[pl.ds(start, bk), :], cp_ref[pl.ds(start, bk), :]], axis=1)

    def v_tile(n):
        start = pl.multiple_of(n * bk, bk)
        return lambda hh: vt_ref[hh * HEAD_DIM:(hh + 1) * HEAD_DIM, pl.ds(start, bk)]

    def k_meta():
        return jnp.concatenate([km_ref[...], cpm_ref[...]], axis=1)

    def v_meta(hh):
        return vtm_ref[hh * HEAD_DIM:(hh + 1) * HEAD_DIM, :]

    def stage(k_aug, s_slot, v_blk, u_slot, diagonal=False):
        for hh in heads:
            for c in range(bq // Q_CHUNK):
                cols = slice(c * Q_CHUNK, (c + 1) * Q_CHUNK)
                if k_aug is not None:
                    scores(k_aug, s_slot, hh, cols, diagonal)
                if v_blk is not None:
                    update(v_blk(hh), u_slot, hh, cols)

    def tile_at(j):
        return jnp.where(j == 0, qi, j - 1)

    @pl.when(qi == 0)
    def _():
        build_q_aug(0)
        stage(k_tile(0), 0, None, None, diagonal=True)
        stage(k_meta(), meta_slot, None, None)

    for hh in heads:
        m_ref[hh] = jnp.full(m_ref.shape[1:], MASKED, jnp.float32)
        acc_ref[hh] = jnp.zeros(acc_ref.shape[1:], jnp.float32)

    first_dyn = lax.shift_right_logical(qi + 1, 1) & 1
    for first in range(2):
        @pl.when(first_dyn == first)
        def _(first=first):
            def two_stages(i, carry):
                j = 2 * i
                stage(k_tile(j), 1 - first, v_tile(tile_at(j)), first)
                stage(k_tile(j + 1), first, v_tile(j), 1 - first)
                return carry

            lax.fori_loop(0, lax.shift_right_logical(qi, 1), two_stages, 0)

    next_blk = jnp.minimum(qi + 1, n_q - 1)
    for variant in range(4):
        @pl.when((qi & 3) == variant)
        def _(variant=variant):
            first = ((variant + 1) >> 1) & 1
            last = first
            if variant & 1:
                stage(k_tile(qi - 1), 1 - first, v_tile(tile_at(qi - 1)), first)
                last = 1 - first
            build_q_aug(next_blk)
            stage(k_tile(next_blk), 1 - last, v_tile(tile_at(qi)), last, diagonal=True)
            stage(k_meta(), meta_slot + (variant + 1) % 2, v_meta, meta_slot + variant % 2)

    outs = []
    for hh in range(HEADS_PER_STEP):
        acc = acc_ref[hh]
        outs.append(acc[0:HEAD_DIM] / acc[HEAD_DIM:HEAD_DIM + 1])
    o_ref[...] = _bf16(jnp.concatenate(outs, axis=0).T)


def _out_ffn_kernel(x_ref, a_ref, p_ref, wo_ref, n2w_ref, wg_ref, wu_ref, wd_ref, o_ref):
    mix = _dot(jnp.concatenate([a_ref[...], p_ref[...]], axis=1), wo_ref[...])
    h1 = x_ref[...] + mix
    n2 = _bf16(h1 * _rms_scale(h1) * n2w_ref[...])
    acc = h1
    lo = 0
    for width in FF_CHUNKS:
        g = _dot(n2, wg_ref[:, lo:lo + width])
        u = _dot(n2, wu_ref[:, lo:lo + width])
        acc = acc + _dot(_bf16(g * jax.nn.sigmoid(g) * u), wd_ref[lo:lo + width, :])
        lo += width
    o_ref[...] = acc


def _const_spec(shape, single_buffer=False):
    zeros = (0,) * len(shape)
    if single_buffer:
        return pl.BlockSpec(shape, lambda *_: zeros, pipeline_mode=pl.Buffered(1))
    return pl.BlockSpec(shape, lambda *_: zeros)


def _block_diag(blocks):
    n, r, c = blocks.shape
    out = jnp.zeros((n * r, n * c), blocks.dtype)
    for i in range(n):
        out = out.at[i * r:(i + 1) * r, i * c:(i + 1) * c].set(blocks[i])
    return out


def kernel(x, meta_tokens, norm1_w, w_in, b_fgate, q_norm_w, k_norm_w, w_pool, pool_scale, w_out,
           norm2_w, w_gate, w_up, w_down):
    batch, seq, d_model = x.shape
    assert w_in.shape[0] == 1, "one layer"
    assert seq % INPROJ_BLOCK == 0 and INPROJ_BLOCK % INPROJ_SUB == 0
    assert seq % ROW_BLOCK == 0 and seq % Q_BLOCK == 0 and Q_BLOCK == KV_BLOCK
    d_ff = w_gate.shape[-1]
    assert sum(FF_CHUNKS) == d_ff
    f32, bf16 = jnp.float32, jnp.bfloat16

    w = w_in[0]
    w_q, w_k, w_v = (w[:, i * D_ATTN:(i + 1) * D_ATTN] for i in range(3))
    w_fg = w[:, 3 * D_ATTN:3 * D_ATTN + N_HEADS]
    w_u = w[:, 3 * D_ATTN + N_HEADS:]
    wt = jnp.concatenate([w_q.T, w_v.T, w_fg.T, jnp.zeros((GATE_ROWS - N_HEADS, d_model), f32)],
                         axis=0).astype(bf16)
    wku = jnp.concatenate([w_k, w_u], axis=1).astype(bf16)
    n1w = norm1_w[0].reshape(1, d_model)
    n2w = norm2_w[0].reshape(1, d_model)
    bfg = jnp.broadcast_to(b_fgate[0].reshape(N_HEADS, 1), (N_HEADS, LANES))
    qw = jnp.broadcast_to(jnp.tile(q_norm_w[0] * (HEAD_DIM ** -0.5 * LOG2E), N_HEADS).reshape(D_ATTN, 1),
                          (D_ATTN, LANES))
    kw = jnp.tile(k_norm_w[0], N_HEADS).reshape(1, D_ATTN)
    bd = _block_diag(jnp.ones((MXU_TILE // HEAD_DIM, HEAD_DIM, HEAD_DIM), bf16))
    wp = jnp.stack([_block_diag(w_pool[0, 2 * i:2 * i + 2]) for i in range(D_POOL // MXU_TILE)]
                   ).astype(bf16)
    ps = pool_scale[0].reshape(1, D_POOL)
    wo = w_out[0].astype(bf16)
    wg = w_gate[0].astype(bf16)
    wu = w_up[0].astype(bf16)
    wd = w_down[0].astype(bf16)

    def upper_tri(n):
        return (lax.broadcasted_iota(jnp.int32, (n, n), 0)
                <= lax.broadcasted_iota(jnp.int32, (n, n), 1)).astype(bf16)

    meta_blk = jnp.zeros((META_BLOCK, d_model), f32).at[0:N_META].set(meta_tokens)
    n_proj_t = 2 * D_ATTN + GATE_ROWS

    km, cpm, vtm, um, cm = pl.pallas_call(
        _meta_inproj_kernel,
        out_shape=(jax.ShapeDtypeStruct((META_BLOCK, D_ATTN), bf16),
                   jax.ShapeDtypeStruct((META_BLOCK, LANES), bf16),
                   jax.ShapeDtypeStruct((D_ATTN, META_BLOCK), bf16),
                   jax.ShapeDtypeStruct((N_META, D_POOL), f32),
                   jax.ShapeDtypeStruct((N_HEADS, LANES), f32)),
        compiler_params=pltpu.CompilerParams(vmem_limit_bytes=VMEM_LIMIT_BYTES),
        name="meta_inproj",
    )(meta_blk, n1w, wt, wku, bfg, kw, bd, upper_tri(META_BLOCK))

    tok = lambda b, j: (b, j, 0)
    tok_t = lambda b, j: (b, 0, j)
    qt, k, cp, vt, po = pl.pallas_call(
        _inproj_kernel,
        grid=(batch, seq // INPROJ_BLOCK),
        in_specs=[pl.BlockSpec((None, INPROJ_BLOCK, d_model), tok),
                  _const_spec((N_META, D_POOL)), _const_spec((N_HEADS, LANES)),
                  _const_spec((1, d_model)), _const_spec((n_proj_t, d_model)),
                  _const_spec((d_model, D_ATTN + D_POOL)), _const_spec((N_HEADS, LANES)),
                  _const_spec((D_ATTN, LANES)), _const_spec((1, D_ATTN)),
                  _const_spec((MXU_TILE, MXU_TILE)), _const_spec((INPROJ_SUB, INPROJ_SUB)),
                  _const_spec((D_POOL // MXU_TILE, MXU_TILE, MXU_TILE)), _const_spec((1, D_POOL))],
        out_specs=(pl.BlockSpec((None, D_ATTN, INPROJ_BLOCK), tok_t),
                   pl.BlockSpec((None, INPROJ_BLOCK, D_ATTN), tok),
                   pl.BlockSpec((None, INPROJ_BLOCK, LANES), tok),
                   pl.BlockSpec((None, D_ATTN, INPROJ_BLOCK), tok_t),
                   pl.BlockSpec((None, INPROJ_BLOCK, D_POOL), tok)),
        out_shape=(jax.ShapeDtypeStruct((batch, D_ATTN, seq), bf16),
                   jax.ShapeDtypeStruct((batch, seq, D_ATTN), bf16),
                   jax.ShapeDtypeStruct((batch, seq, LANES), bf16),
                   jax.ShapeDtypeStruct((batch, D_ATTN, seq), bf16),
                   jax.ShapeDtypeStruct((batch, seq, D_POOL), bf16)),
        scratch_shapes=[pltpu.VMEM((N_META + INPROJ_BLOCK, D_POOL), f32),
                        pltpu.VMEM((N_HEADS, LANES), f32)],
        compiler_params=pltpu.CompilerParams(
            dimension_semantics=("arbitrary", "arbitrary"), vmem_limit_bytes=VMEM_LIMIT_BYTES),
        name="inproj",
    )(x, um, cm, n1w, wt, wku, bfg, qw, kw, bd, upper_tri(INPROJ_SUB), wp, ps)

    n_pairs = N_HEADS // HEADS_PER_STEP
    a = pl.pallas_call(
        _attn_kernel,
        grid=(batch, n_pairs, seq // Q_BLOCK),
        in_specs=[pl.BlockSpec((None, LANES, seq), lambda b, p, i: (b, p, 0)),
                  pl.BlockSpec((None, seq, LANES), lambda b, p, i: (b, 0, p)),
                  pl.BlockSpec((None, seq, LANES), lambda b, p, i: (b, 0, 0)),
                  pl.BlockSpec((None, LANES, seq), lambda b, p, i: (b, p, 0)),
                  pl.BlockSpec((META_BLOCK, LANES), lambda b, p, i: (0, p)),
                  pl.BlockSpec((META_BLOCK, LANES), lambda b, p, i: (0, 0)),
                  pl.BlockSpec((LANES, META_BLOCK), lambda b, p, i: (p, 0))],
        out_specs=pl.BlockSpec((None, Q_BLOCK, LANES), lambda b, p, i: (b, i, p)),
        out_shape=jax.ShapeDtypeStruct((batch, seq, D_ATTN), bf16),
        scratch_shapes=[pltpu.VMEM((HEADS_PER_STEP, 2 * LANES, Q_BLOCK), bf16),
                        pltpu.VMEM((4, HEADS_PER_STEP, KV_BLOCK, Q_BLOCK), f32),
                        pltpu.VMEM((4, HEADS_PER_STEP, 1, Q_BLOCK), f32),
                        pltpu.VMEM((HEADS_PER_STEP, 1, Q_BLOCK), f32),
                        pltpu.VMEM((HEADS_PER_STEP, HEAD_DIM + ONES_ROWS, Q_BLOCK), f32)],
        compiler_params=pltpu.CompilerParams(
            dimension_semantics=("parallel", "parallel", "arbitrary"),
            vmem_limit_bytes=VMEM_LIMIT_BYTES),
        name="fox_attn",
    )(qt, k, cp, vt, km, cpm, vtm)

    rows = batch * seq
    row = lambda r: (r, 0)
    out = pl.pallas_call(
        _out_ffn_kernel,
        grid=(rows // ROW_BLOCK,),
        in_specs=[pl.BlockSpec((ROW_BLOCK, d_model), row),
                  pl.BlockSpec((ROW_BLOCK, D_ATTN), row),
                  pl.BlockSpec((ROW_BLOCK, D_POOL), row),
                  _const_spec((D_ATTN + D_POOL, d_model), True), _const_spec((1, d_model)),
                  _const_spec((d_model, d_ff), True), _const_spec((d_model, d_ff), True),
                  _const_spec((d_ff, d_model), True)],
        out_specs=pl.BlockSpec((ROW_BLOCK, d_model), row),
        out_shape=jax.ShapeDtypeStruct((rows, d_model), f32),
        compiler_params=pltpu.CompilerParams(
            dimension_semantics=("parallel",), vmem_limit_bytes=VMEM_LIMIT_BYTES),
        name="out_ffn",
    )(x.reshape(rows, d_model), a.reshape(rows, D_ATTN), po.reshape(rows, D_POOL),
      wo, n2w, wg, wu, wd)
    return out.reshape(batch, seq, d_model)
```

```python
import functools

import jax
import jax.numpy as jnp
from jax import lax
from jax.experimental import pallas as pl
from jax.experimental.pallas import tpu as pltpu

N_META = 16
N_HEADS = 8
HEAD_DIM = 64
D_ATTN = N_HEADS * HEAD_DIM
POOL_WINDOWS = (2, 4, 8, 16)
POOL_GROUP_DIM = 128
D_POOL = len(POOL_WINDOWS) * POOL_GROUP_DIM
EPS = 1e-6

LANES = 128
MXU_TILE = 256
HEADS_PER_STEP = LANES // HEAD_DIM
N_SPLIT = 3
GATE_ROWS = 16
ONES_ROWS = 16
MASKED = -1e30
LOG2E = 1.4426950408889634

META_BLOCK = 128
INPROJ_BLOCK = 1024
INPROJ_SUB = 256
ROW_BLOCK = 512
Q_BLOCK = 512
KV_BLOCK = 512
Q_CHUNK = Q_BLOCK
FF_CHUNKS = (1024, 1024, 768)
VMEM_LIMIT_BYTES = 56 * 1024 * 1024


def _f32(x):
    return x.astype(jnp.float32)


def _bf16(x):
    return x.astype(jnp.bfloat16)


def _dot(a, b):
    return jnp.dot(a, b, preferred_element_type=jnp.float32)


def _dot_nt(a, b):
    return lax.dot_general(a, b, (((1,), (1,)), ((), ())), preferred_element_type=jnp.float32)


def _split_bf16(x, n):
    parts = []
    r = x
    for _ in range(n):
        p = _f32(_bf16(r))
        parts.append(p)
        r = r - p
    return parts


def _rms_scale(x):
    return lax.rsqrt(jnp.mean(x * x, axis=-1, keepdims=True) + EPS)


def _tile_lanes(x, n):
    return jnp.concatenate([x] * (n // x.shape[1]), axis=1) if n != x.shape[1] else x


def _norm_project(h, n1w_ref, wt_ref, wku_ref):
    n1 = _bf16(h * _rms_scale(h) * n1w_ref[...])
    proj_t = _dot_nt(wt_ref[...], n1)
    ku = _dot(n1, wku_ref[...])
    return proj_t, ku


def _key_norm(k, kw_ref, bd_ref):
    hi, lo = (_bf16(p) for p in _split_bf16(k * k, 2))
    bd = bd_ref[...]
    halves = []
    for c in range(D_ATTN // MXU_TILE):
        sl = slice(c * MXU_TILE, (c + 1) * MXU_TILE)
        halves.append(_dot(hi[:, sl], bd) + _dot(lo[:, sl], bd))
    ss = jnp.concatenate(halves, axis=1)
    return k * lax.rsqrt(ss * (1.0 / HEAD_DIM) + EPS) * kw_ref[...]


def _gate_prefix(fg_t, bfg_ref, tri_ref, carry, n_valid):
    t = fg_t.shape[1]
    logf = jax.nn.log_sigmoid(fg_t + _tile_lanes(bfg_ref[...], t))
    if n_valid < t:
        lane = lax.broadcasted_iota(jnp.int32, logf.shape, 1)
        logf = jnp.where(lane < n_valid, logf, 0.0)
    parts = _split_bf16(logf, N_SPLIT)
    pad = jnp.zeros((2 * GATE_ROWS - N_SPLIT * N_HEADS, t), jnp.float32)
    stacked = _bf16(jnp.concatenate(parts + [pad], axis=0))
    cs = _dot(stacked, tri_ref[...])
    c = _tile_lanes(carry, t) + cs[0:N_HEADS]
    for p in range(1, N_SPLIT):
        c = c + cs[p * N_HEADS:(p + 1) * N_HEADS]
    total = jnp.sum(logf, axis=1, keepdims=True)
    return c, total


def _key_bias_columns(c, n_valid):
    t = c.shape[1]
    parts = _split_bf16(c * -LOG2E, N_SPLIT)
    if n_valid < t:
        lane = lax.broadcasted_iota(jnp.int32, c.shape, 1)
        parts[0] = jnp.where(lane < n_valid, parts[0], MASKED)
    pad = jnp.zeros((LANES - N_SPLIT * N_HEADS, t), jnp.float32)
    return _bf16(jnp.concatenate(parts + [pad], axis=0).T)


def _meta_inproj_kernel(h_ref, n1w_ref, wt_ref, wku_ref, bfg_ref, kw_ref, bd_ref, tri_ref,
                        k_ref, cp_ref, vt_ref, u_ref, c_ref):
    proj_t, ku = _norm_project(h_ref[...], n1w_ref, wt_ref, wku_ref)
    vt_ref[...] = _bf16(proj_t[D_ATTN:2 * D_ATTN])
    k_ref[...] = _bf16(_key_norm(ku[:, 0:D_ATTN], kw_ref, bd_ref))
    u_ref[...] = ku[0:N_META, D_ATTN:]
    zero = jnp.zeros((N_HEADS, LANES), jnp.float32)
    c, total = _gate_prefix(proj_t[2 * D_ATTN:2 * D_ATTN + N_HEADS], bfg_ref, tri_ref, zero, N_META)
    cp_ref[...] = _key_bias_columns(c, N_META)
    c_ref[...] = jnp.broadcast_to(total, (N_HEADS, LANES))


def _inproj_kernel(x_ref, um_ref, cm_ref, n1w_ref, wt_ref, wku_ref, bfg_ref, qw_ref, kw_ref, bd_ref,
                   tri_ref, wp_ref, ps_ref,
                   qt_ref, k_ref, cp_ref, vt_ref, po_ref,
                   ubuf_ref, carry_ref):
    t = INPROJ_SUB

    @pl.when(pl.program_id(1) == 0)
    def _():
        ubuf_ref[0:N_META, :] = um_ref[...]
        carry_ref[...] = cm_ref[...]

    carry = carry_ref[...]
    n_sub = x_ref.shape[0] // t

    def project(r):
        return _norm_project(x_ref[r * t:(r + 1) * t, :], n1w_ref, wt_ref, wku_ref)

    projected = project(0)
    for r in range(n_sub):
        rows = slice(r * t, (r + 1) * t)
        proj_t, ku = projected
        if r + 1 < n_sub:
            projected = project(r + 1)
        vt_ref[:, rows] = _bf16(proj_t[D_ATTN:2 * D_ATTN])

        q3 = proj_t[0:D_ATTN].reshape(N_HEADS, HEAD_DIM, t)
        q3 = q3 * lax.rsqrt(jnp.mean(q3 * q3, axis=1, keepdims=True) + EPS)
        qt_ref[:, rows] = _bf16(q3.reshape(D_ATTN, t) * _tile_lanes(qw_ref[...], t))

        k_ref[rows, :] = _bf16(_key_norm(ku[:, 0:D_ATTN], kw_ref, bd_ref))

        c, total = _gate_prefix(proj_t[2 * D_ATTN:2 * D_ATTN + N_HEADS], bfg_ref, tri_ref, carry, t)
        cp_ref[rows, :] = _key_bias_columns(c, t)
        carry = carry + total

        base = N_META + r * t
        ubuf_ref[base:base + t, :] = ku[:, D_ATTN:]
        groups = []
        for g, w in enumerate(POOL_WINDOWS):
            lanes = slice(g * POOL_GROUP_DIM, (g + 1) * POOL_GROUP_DIM)
            cur = ubuf_ref[base:base + t, lanes]
            acc = cur
            for j in range(1, w):
                acc = acc + ubuf_ref[base - j:base - j + t, lanes]
            groups.append(acc * (1.0 / w) - cur)
        pooled = _bf16(jnp.concatenate(groups, axis=1))
        halves = []
        for c2 in range(D_POOL // MXU_TILE):
            halves.append(_dot(pooled[:, c2 * MXU_TILE:(c2 + 1) * MXU_TILE], wp_ref[c2]))
        po_ref[rows, :] = _bf16(jnp.concatenate(halves, axis=1) * ps_ref[...])
    carry_ref[...] = carry
    ubuf_ref[0:N_META, :] = ubuf_ref[x_ref.shape[0]:x_ref.shape[0] + N_META, :]


def _attn_kernel(qt_ref, k_ref, cp_ref, vt_ref, km_ref, cpm_ref, vtm_ref, o_ref,
                 qa_ref, s_ref, cmax_ref, m_ref, acc_ref):
    pair = pl.program_id(1)
    qi = pl.program_id(2)
    n_q = pl.num_programs(2)
    bq = o_ref.shape[0]
    bk = KV_BLOCK
    heads = range(HEADS_PER_STEP)
    ones = jnp.ones((ONES_ROWS, bk), jnp.bfloat16)
    meta_slot = 2

    def build_q_aug(blk, qbuf):
        qt = qt_ref[:, pl.ds(pl.multiple_of(blk * bq, bq), bq)]
        zeros_q = jnp.zeros((HEAD_DIM, bq), qt.dtype)
        row = lax.broadcasted_iota(jnp.int32, (LANES, bq), 0)
        for hh in heads:
            head = pair * HEADS_PER_STEP + hh
            pieces = [qt[r * HEAD_DIM:(r + 1) * HEAD_DIM] if r == hh else zeros_q for r in heads]
            sel = ((row & (N_HEADS - 1)) == head) & (row < N_SPLIT * N_HEADS)
            qa_ref[qbuf, hh] = jnp.concatenate(
                pieces + [jnp.where(sel, 1.0, 0.0).astype(qt.dtype)], axis=0)

    def scores(k_aug, slot, hh, cols, qbuf, diagonal=False):
        rows = k_aug.shape[0]
        s = _dot(k_aug, qa_ref[qbuf, hh, :, cols])
        if diagonal:
            causal = (lax.broadcasted_iota(jnp.int32, s.shape, 0)
                      <= lax.broadcasted_iota(jnp.int32, s.shape, 1) + cols.start)
            s = jnp.where(causal, s, MASKED)
        s_ref[slot, hh, 0:rows, cols] = s
        cmax_ref[slot, hh, :, cols] = jnp.max(s, axis=0, keepdims=True)

    def update(v_blk, slot, hh, cols):
        rows = v_blk.shape[1]
        m_old = m_ref[hh, :, cols]
        m_new = jnp.maximum(m_old, cmax_ref[slot, hh, :, cols])
        p = _bf16(jnp.exp2(s_ref[slot, hh, 0:rows, cols] - m_new))
        v_aug = jnp.concatenate([v_blk, ones[:, 0:rows]], axis=0)
        acc_ref[hh, :, cols] = jnp.exp2(m_old - m_new) * acc_ref[hh, :, cols] + _dot(v_aug, p)
        m_ref[hh, :, cols] = m_new

    def k_tile(n):
        start = pl.multiple_of(n * bk, bk)
        return jnp.concatenate([k_ref[pl.ds(start, bk), :], cp_ref[pl.ds(start, bk), :]], axis=1)

    def v_tile(n):
        start = pl.multiple_of(n * bk, bk)
        return lambda hh: vt_ref[hh * HEAD_DIM:(hh + 1) * HEAD_DIM, pl.ds(start, bk)]

    def k_meta():
        return jnp.concatenate([km_ref[...], cpm_ref[...]], axis=1)

    def v_meta(hh):
        return vtm_ref[hh * HEAD_DIM:(hh + 1) * HEAD_DIM, :]

    def stage(k_aug, s_slot, v_blk, u_slot, qbuf=None, diagonal=False):
        for hh in heads:
            for c in range(bq // Q_CHUNK):
                cols = slice(c * Q_CHUNK, (c + 1) * Q_CHUNK)
                if k_aug is not None:
                    scores(k_aug, s_slot, hh, cols, qbuf, diagonal)
                if v_blk is not None:
                    update(v_blk(hh), u_slot, hh, cols)

    def tile_at(j):
        return jnp.where(j == 0, qi, j - 1)

    @pl.when(qi == 0)
    def _():
        build_q_aug(0, 0)
        stage(k_tile(0), 0, None, None, 0, diagonal=True)
        stage(k_meta(), meta_slot, None, None, 0)

    for hh in heads:
        m_ref[hh] = jnp.full(m_ref.shape[1:], MASKED, jnp.float32)
        acc_ref[hh] = jnp.zeros(acc_ref.shape[1:], jnp.float32)

    first_dyn = lax.shift_right_logical(qi + 1, 1) & 1
    parity_dyn = qi & 1
    for first in range(2):
        @pl.when(first_dyn == first)
        def _(first=first):
            def two_stages(i, carry):
                j = 2 * i
                stage(k_tile(j), 1 - first, v_tile(tile_at(j)), first, parity_dyn)
                stage(k_tile(j + 1), first, v_tile(j), 1 - first, parity_dyn)
                return carry

            lax.fori_loop(0, lax.shift_right_logical(qi, 1), two_stages, 0)

    next_blk = jnp.minimum(qi + 1, n_q - 1)
    for variant in range(4):
        @pl.when((qi & 3) == variant)
        def _(variant=variant):
            first = ((variant + 1) >> 1) & 1
            parity = variant % 2
            build_q_aug(next_blk, 1 - parity)
            last = first
            if parity:
                stage(k_tile(qi - 1), 1 - first, v_tile(tile_at(qi - 1)), first, parity)
                last = 1 - first
            stage(k_tile(next_blk), 1 - last, v_tile(tile_at(qi)), last, 1 - parity, diagonal=True)
            stage(k_meta(), meta_slot + 1 - parity, v_meta, meta_slot + parity, 1 - parity)

    outs = []
    for hh in range(HEADS_PER_STEP):
        acc = acc_ref[hh]
        outs.append(acc[0:HEAD_DIM] / acc[HEAD_DIM:HEAD_DIM + 1])
    o_ref[...] = _bf16(jnp.concatenate(outs, axis=0).T)


def _out_ffn_kernel(x_ref, a_ref, p_ref, wo_ref, n2w_ref, wg_ref, wu_ref, wd_ref, o_ref):
    mix = _dot(jnp.concatenate([a_ref[...], p_ref[...]], axis=1), wo_ref[...])
    h1 = x_ref[...] + mix
    n2 = _bf16(h1 * _rms_scale(h1) * n2w_ref[...])
    acc = h1
    lo = 0
    for width in FF_CHUNKS:
        g = _dot(n2, wg_ref[:, lo:lo + width])
        u = _dot(n2, wu_ref[:, lo:lo + width])
        acc = acc + _dot(_bf16(g * jax.nn.sigmoid(g) * u), wd_ref[lo:lo + width, :])
        lo += width
    o_ref[...] = acc


def _const_spec(shape, single_buffer=False):
    zeros = (0,) * len(shape)
    if single_buffer:
        return pl.BlockSpec(shape, lambda *_: zeros, pipeline_mode=pl.Buffered(1))
    return pl.BlockSpec(shape, lambda *_: zeros)


def _block_diag(blocks):
    n, r, c = blocks.shape
    out = jnp.zeros((n * r, n * c), blocks.dtype)
    for i in range(n):
        out = out.at[i * r:(i + 1) * r, i * c:(i + 1) * c].set(blocks[i])
    return out


def kernel(x, meta_tokens, norm1_w, w_in, b_fgate, q_norm_w, k_norm_w, w_pool, pool_scale, w_out,
           norm2_w, w_gate, w_up, w_down):
    batch, seq, d_model = x.shape
    assert w_in.shape[0] == 1, "one layer"
    assert seq % INPROJ_BLOCK == 0 and INPROJ_BLOCK % INPROJ_SUB == 0
    assert seq % ROW_BLOCK == 0 and seq % Q_BLOCK == 0 and Q_BLOCK == KV_BLOCK
    d_ff = w_gate.shape[-1]
    assert sum(FF_CHUNKS) == d_ff
    f32, bf16 = jnp.float32, jnp.bfloat16

    w = w_in[0]
    w_q, w_k, w_v = (w[:, i * D_ATTN:(i + 1) * D_ATTN] for i in range(3))
    w_fg = w[:, 3 * D_ATTN:3 * D_ATTN + N_HEADS]
    w_u = w[:, 3 * D_ATTN + N_HEADS:]
    wt = jnp.concatenate([w_q.T, w_v.T, w_fg.T, jnp.zeros((GATE_ROWS - N_HEADS, d_model), f32)],
                         axis=0).astype(bf16)
    wku = jnp.concatenate([w_k, w_u], axis=1).astype(bf16)
    n1w = norm1_w[0].reshape(1, d_model)
    n2w = norm2_w[0].reshape(1, d_model)
    bfg = jnp.broadcast_to(b_fgate[0].reshape(N_HEADS, 1), (N_HEADS, LANES))
    qw = jnp.broadcast_to(jnp.tile(q_norm_w[0] * (HEAD_DIM ** -0.5 * LOG2E), N_HEADS).reshape(D_ATTN, 1),
                          (D_ATTN, LANES))
    kw = jnp.tile(k_norm_w[0], N_HEADS).reshape(1, D_ATTN)
    bd = _block_diag(jnp.ones((MXU_TILE // HEAD_DIM, HEAD_DIM, HEAD_DIM), bf16))
    wp = jnp.stack([_block_diag(w_pool[0, 2 * i:2 * i + 2]) for i in range(D_POOL // MXU_TILE)]
                   ).astype(bf16)
    ps = pool_scale[0].reshape(1, D_POOL)
    wo = w_out[0].astype(bf16)
    wg = w_gate[0].astype(bf16)
    wu = w_up[0].astype(bf16)
    wd = w_down[0].astype(bf16)

    def upper_tri(n):
        return (lax.broadcasted_iota(jnp.int32, (n, n), 0)
                <= lax.broadcasted_iota(jnp.int32, (n, n), 1)).astype(bf16)

    meta_blk = jnp.zeros((META_BLOCK, d_model), f32).at[0:N_META].set(meta_tokens)
    n_proj_t = 2 * D_ATTN + GATE_ROWS

    km, cpm, vtm, um, cm = pl.pallas_call(
        _meta_inproj_kernel,
        out_shape=(jax.ShapeDtypeStruct((META_BLOCK, D_ATTN), bf16),
                   jax.ShapeDtypeStruct((META_BLOCK, LANES), bf16),
                   jax.ShapeDtypeStruct((D_ATTN, META_BLOCK), bf16),
                   jax.ShapeDtypeStruct((N_META, D_POOL), f32),
                   jax.ShapeDtypeStruct((N_HEADS, LANES), f32)),
        compiler_params=pltpu.CompilerParams(vmem_limit_bytes=VMEM_LIMIT_BYTES),
        name="meta_inproj",
    )(meta_blk, n1w, wt, wku, bfg, kw, bd, upper_tri(META_BLOCK))

    tok = lambda b, j: (b, j, 0)
    tok_t = lambda b, j: (b, 0, j)
    qt, k, cp, vt, po = pl.pallas_call(
        _inproj_kernel,
        grid=(batch, seq // INPROJ_BLOCK),
        in_specs=[pl.BlockSpec((None, INPROJ_BLOCK, d_model), tok),
                  _const_spec((N_META, D_POOL)), _const_spec((N_HEADS, LANES)),
                  _const_spec((1, d_model)), _const_spec((n_proj_t, d_model)),
                  _const_spec((d_model, D_ATTN + D_POOL)), _const_spec((N_HEADS, LANES)),
                  _const_spec((D_ATTN, LANES)), _const_spec((1, D_ATTN)),
                  _const_spec((MXU_TILE, MXU_TILE)), _const_spec((INPROJ_SUB, INPROJ_SUB)),
                  _const_spec((D_POOL // MXU_TILE, MXU_TILE, MXU_TILE)), _const_spec((1, D_POOL))],
        out_specs=(pl.BlockSpec((None, D_ATTN, INPROJ_BLOCK), tok_t),
                   pl.BlockSpec((None, INPROJ_BLOCK, D_ATTN), tok),
                   pl.BlockSpec((None, INPROJ_BLOCK, LANES), tok),
                   pl.BlockSpec((None, D_ATTN, INPROJ_BLOCK), tok_t),
                   pl.BlockSpec((None, INPROJ_BLOCK, D_POOL), tok)),
        out_shape=(jax.ShapeDtypeStruct((batch, D_ATTN, seq), bf16),
                   jax.ShapeDtypeStruct((batch, seq, D_ATTN), bf16),
                   jax.ShapeDtypeStruct((batch, seq, LANES), bf16),
                   jax.ShapeDtypeStruct((batch, D_ATTN, seq), bf16),
                   jax.ShapeDtypeStruct((batch, seq, D_POOL), bf16)),
        scratch_shapes=[pltpu.VMEM((N_META + INPROJ_BLOCK, D_POOL), f32),
                        pltpu.VMEM((N_HEADS, LANES), f32)],
        compiler_params=pltpu.CompilerParams(
            dimension_semantics=("arbitrary", "arbitrary"), vmem_limit_bytes=VMEM_LIMIT_BYTES),
        name="inproj",
    )(x, um, cm, n1w, wt, wku, bfg, qw, kw, bd, upper_tri(INPROJ_SUB), wp, ps)

    n_pairs = N_HEADS // HEADS_PER_STEP
    a = pl.pallas_call(
        _attn_kernel,
        grid=(batch, n_pairs, seq // Q_BLOCK),
        in_specs=[pl.BlockSpec((None, LANES, seq), lambda b, p, i: (b, p, 0)),
                  pl.BlockSpec((None, seq, LANES), lambda b, p, i: (b, 0, p)),
                  pl.BlockSpec((None, seq, LANES), lambda b, p, i: (b, 0, 0)),
                  pl.BlockSpec((None, LANES, seq), lambda b, p, i: (b, p, 0)),
                  pl.BlockSpec((META_BLOCK, LANES), lambda b, p, i: (0, p)),
                  pl.BlockSpec((META_BLOCK, LANES), lambda b, p, i: (0, 0)),
                  pl.BlockSpec((LANES, META_BLOCK), lambda b, p, i: (p, 0))],
        out_specs=pl.BlockSpec((None, Q_BLOCK, LANES), lambda b, p, i: (b, i, p)),
        out_shape=jax.ShapeDtypeStruct((batch, seq, D_ATTN), bf16),
        scratch_shapes=[pltpu.VMEM((2, HEADS_PER_STEP, 2 * LANES, Q_BLOCK), bf16),
                        pltpu.VMEM((4, HEADS_PER_STEP, KV_BLOCK, Q_BLOCK), f32),
                        pltpu.VMEM((4, HEADS_PER_STEP, 1, Q_BLOCK), f32),
                        pltpu.VMEM((HEADS_PER_STEP, 1, Q_BLOCK), f32),
                        pltpu.VMEM((HEADS_PER_STEP, HEAD_DIM + ONES_ROWS, Q_BLOCK), f32)],
        compiler_params=pltpu.CompilerParams(
            dimension_semantics=("parallel", "parallel", "arbitrary"),
            vmem_limit_bytes=VMEM_LIMIT_BYTES),
        name="fox_attn",
    )(qt, k, cp, vt, km, cpm, vtm)

    rows = batch * seq
    row = lambda r: (r, 0)
    out = pl.pallas_call(
        _out_ffn_kernel,
        grid=(rows // ROW_BLOCK,),
        in_specs=[pl.BlockSpec((ROW_BLOCK, d_model), row),
                  pl.BlockSpec((ROW_BLOCK, D_ATTN), row),
                  pl.BlockSpec((ROW_BLOCK, D_POOL), row),
                  _const_spec((D_ATTN + D_POOL, d_model), True), _const_spec((1, d_model)),
                  _const_spec((d_model, d_ff), True), _const_spec((d_model, d_ff), True),
                  _const_spec((d_ff, d_model), True)],
        out_specs=pl.BlockSpec((ROW_BLOCK, d_model), row),
        out_shape=jax.ShapeDtypeStruct((rows, d_model), f32),
        compiler_params=pltpu.CompilerParams(
            dimension_semantics=("parallel",), vmem_limit_bytes=VMEM_LIMIT_BYTES),
        name="out_ffn",
    )(x.reshape(rows, d_model), a.reshape(rows, D_ATTN), po.reshape(rows, D_POOL),
      wo, n2w, wg, wu, wd)
    return out.reshape(batch, seq, d_model)
```

```python
import functools

import jax
import jax.numpy as jnp
from jax import lax
from jax.experimental import pallas as pl
from jax.experimental.pallas import tpu as pltpu

N_META = 16
N_HEADS = 8
HEAD_DIM = 64
D_ATTN = N_HEADS * HEAD_DIM
POOL_WINDOWS = (2, 4, 8, 16)
POOL_GROUP_DIM = 128
D_POOL = len(POOL_WINDOWS) * POOL_GROUP_DIM
EPS = 1e-6

LANES = 128
MXU_TILE = 256
HEADS_PER_STEP = LANES // HEAD_DIM
N_SPLIT = 3
GATE_ROWS = 16
ONES_ROWS = 16
MASKED = -1e30
LOG2E = 1.4426950408889634

META_BLOCK = 128
INPROJ_BLOCK = 1024
INPROJ_SUB = 256
ROW_BLOCK = 512
Q_BLOCK = 512
KV_BLOCK = 512
Q_CHUNK = Q_BLOCK
LOOP_STAGES = 4
FF_CHUNKS = (1024, 1024, 768)
VMEM_LIMIT_BYTES = 56 * 1024 * 1024


def _f32(x):
    return x.astype(jnp.float32)


def _bf16(x):
    return x.astype(jnp.bfloat16)


def _dot(a, b):
    return jnp.dot(a, b, preferred_element_type=jnp.float32)


def _dot_nt(a, b):
    return lax.dot_general(a, b, (((1,), (1,)), ((), ())), preferred_element_type=jnp.float32)


def _split_bf16(x, n):
    parts = []
    r = x
    for _ in range(n):
        p = _f32(_bf16(r))
        parts.append(p)
        r = r - p
    return parts


def _rms_scale(x):
    return lax.rsqrt(jnp.mean(x * x, axis=-1, keepdims=True) + EPS)


def _tile_lanes(x, n):
    return jnp.concatenate([x] * (n // x.shape[1]), axis=1) if n != x.shape[1] else x


def _norm_project(h, n1w_ref, wt_ref, wku_ref):
    n1 = _bf16(h * _rms_scale(h) * n1w_ref[...])
    proj_t = _dot_nt(wt_ref[...], n1)
    ku = _dot(n1, wku_ref[...])
    return proj_t, ku


def _key_norm(k, kw_ref, bd_ref):
    hi, lo = (_bf16(p) for p in _split_bf16(k * k, 2))
    bd = bd_ref[...]
    halves = []
    for c in range(D_ATTN // MXU_TILE):
        sl = slice(c * MXU_TILE, (c + 1) * MXU_TILE)
        halves.append(_dot(hi[:, sl], bd) + _dot(lo[:, sl], bd))
    ss = jnp.concatenate(halves, axis=1)
    return k * lax.rsqrt(ss * (1.0 / HEAD_DIM) + EPS) * kw_ref[...]


def _gate_prefix(fg_t, bfg_ref, tri_ref, carry, n_valid):
    t = fg_t.shape[1]
    logf = jax.nn.log_sigmoid(fg_t + _tile_lanes(bfg_ref[...], t))
    if n_valid < t:
        lane = lax.broadcasted_iota(jnp.int32, logf.shape, 1)
        logf = jnp.where(lane < n_valid, logf, 0.0)
    parts = _split_bf16(logf, N_SPLIT)
    pad = jnp.zeros((2 * GATE_ROWS - N_SPLIT * N_HEADS, t), jnp.float32)
    stacked = _bf16(jnp.concatenate(parts + [pad], axis=0))
    cs = _dot(stacked, tri_ref[...])
    c = _tile_lanes(carry, t) + cs[0:N_HEADS]
    for p in range(1, N_SPLIT):
        c = c + cs[p * N_HEADS:(p + 1) * N_HEADS]
    total = jnp.sum(logf, axis=1, keepdims=True)
    return c, total


def _key_bias_columns(c, n_valid):
    t = c.shape[1]
    parts = _split_bf16(c * -LOG2E, N_SPLIT)
    if n_valid < t:
        lane = lax.broadcasted_iota(jnp.int32, c.shape, 1)
        parts[0] = jnp.where(lane < n_valid, parts[0], MASKED)
    pad = jnp.zeros((LANES - N_SPLIT * N_HEADS, t), jnp.float32)
    return _bf16(jnp.concatenate(parts + [pad], axis=0).T)


def _meta_inproj_kernel(h_ref, n1w_ref, wt_ref, wku_ref, bfg_ref, kw_ref, bd_ref, tri_ref,
                        k_ref, cp_ref, vt_ref, u_ref, c_ref):
    proj_t, ku = _norm_project(h_ref[...], n1w_ref, wt_ref, wku_ref)
    vt_ref[...] = _bf16(proj_t[D_ATTN:2 * D_ATTN])
    k_ref[...] = _bf16(_key_norm(ku[:, 0:D_ATTN], kw_ref, bd_ref))
    u_ref[...] = ku[0:N_META, D_ATTN:]
    zero = jnp.zeros((N_HEADS, LANES), jnp.float32)
    c, total = _gate_prefix(proj_t[2 * D_ATTN:2 * D_ATTN + N_HEADS], bfg_ref, tri_ref, zero, N_META)
    cp_ref[...] = _key_bias_columns(c, N_META)
    c_ref[...] = jnp.broadcast_to(total, (N_HEADS, LANES))


def _inproj_kernel(x_ref, um_ref, cm_ref, n1w_ref, wt_ref, wku_ref, bfg_ref, qw_ref, kw_ref, bd_ref,
                   tri_ref, wp_ref, ps_ref,
                   qt_ref, k_ref, cp_ref, vt_ref, po_ref,
                   ubuf_ref, carry_ref):
    t = INPROJ_SUB

    @pl.when(pl.program_id(1) == 0)
    def _():
        ubuf_ref[0:N_META, :] = um_ref[...]
        carry_ref[...] = cm_ref[...]

    carry = carry_ref[...]
    n_sub = x_ref.shape[0] // t

    def project(r):
        return _norm_project(x_ref[r * t:(r + 1) * t, :], n1w_ref, wt_ref, wku_ref)

    projected = project(0)
    for r in range(n_sub):
        rows = slice(r * t, (r + 1) * t)
        proj_t, ku = projected
        if r + 1 < n_sub:
            projected = project(r + 1)
        vt_ref[:, rows] = _bf16(proj_t[D_ATTN:2 * D_ATTN])

        q3 = proj_t[0:D_ATTN].reshape(N_HEADS, HEAD_DIM, t)
        q3 = q3 * lax.rsqrt(jnp.mean(q3 * q3, axis=1, keepdims=True) + EPS)
        qt_ref[:, rows] = _bf16(q3.reshape(D_ATTN, t) * _tile_lanes(qw_ref[...], t))

        k_ref[rows, :] = _bf16(_key_norm(ku[:, 0:D_ATTN], kw_ref, bd_ref))

        c, total = _gate_prefix(proj_t[2 * D_ATTN:2 * D_ATTN + N_HEADS], bfg_ref, tri_ref, carry, t)
        cp_ref[rows, :] = _key_bias_columns(c, t)
        carry = carry + total

        base = N_META + r * t
        ubuf_ref[base:base + t, :] = ku[:, D_ATTN:]
        groups = []
        for g, w in enumerate(POOL_WINDOWS):
            lanes = slice(g * POOL_GROUP_DIM, (g + 1) * POOL_GROUP_DIM)
            cur = ubuf_ref[base:base + t, lanes]
            acc = cur
            for j in range(1, w):
                acc = acc + ubuf_ref[base - j:base - j + t, lanes]
            groups.append(acc * (1.0 / w) - cur)
        pooled = _bf16(jnp.concatenate(groups, axis=1))
        halves = []
        for c2 in range(D_POOL // MXU_TILE):
            halves.append(_dot(pooled[:, c2 * MXU_TILE:(c2 + 1) * MXU_TILE], wp_ref[c2]))
        po_ref[rows, :] = _bf16(jnp.concatenate(halves, axis=1) * ps_ref[...])
    carry_ref[...] = carry
    ubuf_ref[0:N_META, :] = ubuf_ref[x_ref.shape[0]:x_ref.shape[0] + N_META, :]


def _attn_kernel(qt_ref, k_ref, cp_ref, vt_ref, km_ref, cpm_ref, vtm_ref, o_ref,
                 qa_ref, s_ref, cmax_ref, m_ref, acc_ref):
    pair = pl.program_id(1)
    bq = Q_BLOCK
    bk = KV_BLOCK
    n_q = o_ref.shape[0] // bq
    heads = range(HEADS_PER_STEP)
    ones = jnp.ones((ONES_ROWS, bk), jnp.bfloat16)
    meta_slot = 2

    def build_q_aug(blk, qbuf):
        qt = qt_ref[:, pl.ds(pl.multiple_of(blk * bq, bq), bq)]
        zeros_q = jnp.zeros((HEAD_DIM, bq), qt.dtype)
        row = lax.broadcasted_iota(jnp.int32, (LANES, bq), 0)
        for hh in heads:
            head = pair * HEADS_PER_STEP + hh
            pieces = [qt[r * HEAD_DIM:(r + 1) * HEAD_DIM] if r == hh else zeros_q for r in heads]
            sel = ((row & (N_HEADS - 1)) == head) & (row < N_SPLIT * N_HEADS)
            qa_ref[qbuf, hh] = jnp.concatenate(
                pieces + [jnp.where(sel, 1.0, 0.0).astype(qt.dtype)], axis=0)

    def scores(k_aug, slot, hh, cols, qbuf, diagonal=False):
        rows = k_aug.shape[0]
        s = _dot(k_aug, qa_ref[qbuf, hh, :, cols])
        if diagonal:
            causal = (lax.broadcasted_iota(jnp.int32, s.shape, 0)
                      <= lax.broadcasted_iota(jnp.int32, s.shape, 1) + cols.start)
            s = jnp.where(causal, s, MASKED)
        s_ref[slot, hh, 0:rows, cols] = s
        cmax_ref[slot, hh, :, cols] = jnp.max(s, axis=0, keepdims=True)

    def update(v_blk, slot, hh, cols):
        rows = v_blk.shape[1]
        m_old = m_ref[hh, :, cols]
        m_new = jnp.maximum(m_old, cmax_ref[slot, hh, :, cols])
        p = _bf16(jnp.exp2(s_ref[slot, hh, 0:rows, cols] - m_new))
        v_aug = jnp.concatenate([v_blk, ones[:, 0:rows]], axis=0)
        acc_ref[hh, :, cols] = jnp.exp2(m_old - m_new) * acc_ref[hh, :, cols] + _dot(v_aug, p)
        m_ref[hh, :, cols] = m_new

    def k_tile(n):
        start = pl.multiple_of(n * bk, bk)
        return jnp.concatenate([k_ref[pl.ds(start, bk), :], cp_ref[pl.ds(start, bk), :]], axis=1)

    def v_tile(n):
        start = pl.multiple_of(n * bk, bk)
        return lambda hh: vt_ref[hh * HEAD_DIM:(hh + 1) * HEAD_DIM, pl.ds(start, bk)]

    def k_meta():
        return jnp.concatenate([km_ref[...], cpm_ref[...]], axis=1)

    def v_meta(hh):
        return vtm_ref[hh * HEAD_DIM:(hh + 1) * HEAD_DIM, :]

    def stage(k_aug, s_slot, v_blk, u_slot, qbuf=None, diagonal=False):
        for hh in heads:
            for c in range(bq // Q_CHUNK):
                cols = slice(c * Q_CHUNK, (c + 1) * Q_CHUNK)
                if k_aug is not None:
                    scores(k_aug, s_slot, hh, cols, qbuf, diagonal)
                if v_blk is not None:
                    update(v_blk(hh), u_slot, hh, cols)

    build_q_aug(0, 0)
    stage(k_tile(0), 0, None, None, 0, diagonal=True)
    stage(k_meta(), meta_slot, None, None, 0)

    def query_block(qi, carry):
        def tile_at(j):
            return jnp.where(j == 0, qi, j - 1)

        for hh in heads:
            m_ref[hh] = jnp.full(m_ref.shape[1:], MASKED, jnp.float32)
            acc_ref[hh] = jnp.zeros(acc_ref.shape[1:], jnp.float32)

        first_dyn = lax.shift_right_logical(qi + 1, 1) & 1
        parity_dyn = qi & 1
        for first in range(2):
            @pl.when(first_dyn == first)
            def _(first=first):
                def run_stages(j0, count):
                    for t in range(count):
                        slots = (1 - first, first) if t % 2 == 0 else (first, 1 - first)
                        tile = tile_at(j0) if t == 0 else j0 + t - 1
                        stage(k_tile(j0 + t), slots[0], v_tile(tile), slots[1], parity_dyn)

                n_long = lax.shift_right_logical(qi, 2)
                lax.fori_loop(0, n_long, lambda i, c: (run_stages(LOOP_STAGES * i, LOOP_STAGES), c)[1], 0)
                lax.fori_loop(0, lax.shift_right_logical(qi, 1) & 1,
                              lambda i, c: (run_stages(LOOP_STAGES * n_long, 2), c)[1], 0)

        next_blk = jnp.minimum(qi + 1, n_q - 1)
        for variant in range(4):
            @pl.when((qi & 3) == variant)
            def _(variant=variant):
                first = ((variant + 1) >> 1) & 1
                parity = variant % 2
                build_q_aug(next_blk, 1 - parity)
                last = first
                if parity:
                    stage(k_tile(qi - 1), 1 - first, v_tile(tile_at(qi - 1)), first, parity)
                    last = 1 - first
                stage(k_tile(next_blk), 1 - last, v_tile(tile_at(qi)), last, 1 - parity,
                      diagonal=True)
                stage(k_meta(), meta_slot + 1 - parity, v_meta, meta_slot + parity, 1 - parity)

        outs = []
        for hh in heads:
            acc = acc_ref[hh]
            outs.append(acc[0:HEAD_DIM] / acc[HEAD_DIM:HEAD_DIM + 1])
        o_ref[pl.ds(pl.multiple_of(qi * bq, bq), bq), :] = _bf16(jnp.concatenate(outs, axis=0).T)
        return carry

    lax.fori_loop(0, n_q, query_block, 0)


def _out_ffn_kernel(x_ref, a_ref, p_ref, wo_ref, n2w_ref, wg_ref, wu_ref, wd_ref, o_ref):
    mix = _dot(jnp.concatenate([a_ref[...], p_ref[...]], axis=1), wo_ref[...])
    h1 = x_ref[...] + mix
    n2 = _bf16(h1 * _rms_scale(h1) * n2w_ref[...])
    acc = h1
    lo = 0
    for width in FF_CHUNKS:
        g = _dot(n2, wg_ref[:, lo:lo + width])
        u = _dot(n2, wu_ref[:, lo:lo + width])
        acc = acc + _dot(_bf16(g * jax.nn.sigmoid(g) * u), wd_ref[lo:lo + width, :])
        lo += width
    o_ref[...] = acc


def _const_spec(shape, single_buffer=False):
    zeros = (0,) * len(shape)
    if single_buffer:
        return pl.BlockSpec(shape, lambda *_: zeros, pipeline_mode=pl.Buffered(1))
    return pl.BlockSpec(shape, lambda *_: zeros)


def _block_diag(blocks):
    n, r, c = blocks.shape
    out = jnp.zeros((n * r, n * c), blocks.dtype)
    for i in range(n):
        out = out.at[i * r:(i + 1) * r, i * c:(i + 1) * c].set(blocks[i])
    return out


def kernel(x, meta_tokens, norm1_w, w_in, b_fgate, q_norm_w, k_norm_w, w_pool, pool_scale, w_out,
           norm2_w, w_gate, w_up, w_down):
    batch, seq, d_model = x.shape
    assert w_in.shape[0] == 1, "one layer"
    assert seq % INPROJ_BLOCK == 0 and INPROJ_BLOCK % INPROJ_SUB == 0
    assert seq % ROW_BLOCK == 0 and seq % Q_BLOCK == 0 and Q_BLOCK == KV_BLOCK
    d_ff = w_gate.shape[-1]
    assert sum(FF_CHUNKS) == d_ff
    f32, bf16 = jnp.float32, jnp.bfloat16

    w = w_in[0]
    w_q, w_k, w_v = (w[:, i * D_ATTN:(i + 1) * D_ATTN] for i in range(3))
    w_fg = w[:, 3 * D_ATTN:3 * D_ATTN + N_HEADS]
    w_u = w[:, 3 * D_ATTN + N_HEADS:]
    wt = jnp.concatenate([w_q.T, w_v.T, w_fg.T, jnp.zeros((GATE_ROWS - N_HEADS, d_model), f32)],
                         axis=0).astype(bf16)
    wku = jnp.concatenate([w_k, w_u], axis=1).astype(bf16)
    n1w = norm1_w[0].reshape(1, d_model)
    n2w = norm2_w[0].reshape(1, d_model)
    bfg = jnp.broadcast_to(b_fgate[0].reshape(N_HEADS, 1), (N_HEADS, LANES))
    qw = jnp.broadcast_to(jnp.tile(q_norm_w[0] * (HEAD_DIM ** -0.5 * LOG2E), N_HEADS).reshape(D_ATTN, 1),
                          (D_ATTN, LANES))
    kw = jnp.tile(k_norm_w[0], N_HEADS).reshape(1, D_ATTN)
    bd = _block_diag(jnp.ones((MXU_TILE // HEAD_DIM, HEAD_DIM, HEAD_DIM), bf16))
    wp = jnp.stack([_block_diag(w_pool[0, 2 * i:2 * i + 2]) for i in range(D_POOL // MXU_TILE)]
                   ).astype(bf16)
    ps = pool_scale[0].reshape(1, D_POOL)
    wo = w_out[0].astype(bf16)
    wg = w_gate[0].astype(bf16)
    wu = w_up[0].astype(bf16)
    wd = w_down[0].astype(bf16)

    def upper_tri(n):
        return (lax.broadcasted_iota(jnp.int32, (n, n), 0)
                <= lax.broadcasted_iota(jnp.int32, (n, n), 1)).astype(bf16)

    meta_blk = jnp.zeros((META_BLOCK, d_model), f32).at[0:N_META].set(meta_tokens)
    n_proj_t = 2 * D_ATTN + GATE_ROWS

    km, cpm, vtm, um, cm = pl.pallas_call(
        _meta_inproj_kernel,
        out_shape=(jax.ShapeDtypeStruct((META_BLOCK, D_ATTN), bf16),
                   jax.ShapeDtypeStruct((META_BLOCK, LANES), bf16),
                   jax.ShapeDtypeStruct((D_ATTN, META_BLOCK), bf16),
                   jax.ShapeDtypeStruct((N_META, D_POOL), f32),
                   jax.ShapeDtypeStruct((N_HEADS, LANES), f32)),
        compiler_params=pltpu.CompilerParams(vmem_limit_bytes=VMEM_LIMIT_BYTES),
        name="meta_inproj",
    )(meta_blk, n1w, wt, wku, bfg, kw, bd, upper_tri(META_BLOCK))

    tok = lambda b, j: (b, j, 0)
    tok_t = lambda b, j: (b, 0, j)
    qt, k, cp, vt, po = pl.pallas_call(
        _inproj_kernel,
        grid=(batch, seq // INPROJ_BLOCK),
        in_specs=[pl.BlockSpec((None, INPROJ_BLOCK, d_model), tok),
                  _const_spec((N_META, D_POOL)), _const_spec((N_HEADS, LANES)),
                  _const_spec((1, d_model)), _const_spec((n_proj_t, d_model)),
                  _const_spec((d_model, D_ATTN + D_POOL)), _const_spec((N_HEADS, LANES)),
                  _const_spec((D_ATTN, LANES)), _const_spec((1, D_ATTN)),
                  _const_spec((MXU_TILE, MXU_TILE)), _const_spec((INPROJ_SUB, INPROJ_SUB)),
                  _const_spec((D_POOL // MXU_TILE, MXU_TILE, MXU_TILE)), _const_spec((1, D_POOL))],
        out_specs=(pl.BlockSpec((None, D_ATTN, INPROJ_BLOCK), tok_t),
                   pl.BlockSpec((None, INPROJ_BLOCK, D_ATTN), tok),
                   pl.BlockSpec((None, INPROJ_BLOCK, LANES), tok),
                   pl.BlockSpec((None, D_ATTN, INPROJ_BLOCK), tok_t),
                   pl.BlockSpec((None, INPROJ_BLOCK, D_POOL), tok)),
        out_shape=(jax.ShapeDtypeStruct((batch, D_ATTN, seq), bf16),
                   jax.ShapeDtypeStruct((batch, seq, D_ATTN), bf16),
                   jax.ShapeDtypeStruct((batch, seq, LANES), bf16),
                   jax.ShapeDtypeStruct((batch, D_ATTN, seq), bf16),
                   jax.ShapeDtypeStruct((batch, seq, D_POOL), bf16)),
        scratch_shapes=[pltpu.VMEM((N_META + INPROJ_BLOCK, D_POOL), f32),
                        pltpu.VMEM((N_HEADS, LANES), f32)],
        compiler_params=pltpu.CompilerParams(
            dimension_semantics=("arbitrary", "arbitrary"), vmem_limit_bytes=VMEM_LIMIT_BYTES),
        name="inproj",
    )(x, um, cm, n1w, wt, wku, bfg, qw, kw, bd, upper_tri(INPROJ_SUB), wp, ps)

    n_pairs = N_HEADS // HEADS_PER_STEP
    a = pl.pallas_call(
        _attn_kernel,
        grid=(batch, n_pairs),
        in_specs=[pl.BlockSpec((None, LANES, seq), lambda b, p: (b, p, 0)),
                  pl.BlockSpec((None, seq, LANES), lambda b, p: (b, 0, p)),
                  pl.BlockSpec((None, seq, LANES), lambda b, p: (b, 0, 0)),
                  pl.BlockSpec((None, LANES, seq), lambda b, p: (b, p, 0)),
                  pl.BlockSpec((META_BLOCK, LANES), lambda b, p: (0, p)),
                  pl.BlockSpec((META_BLOCK, LANES), lambda b, p: (0, 0)),
                  pl.BlockSpec((LANES, META_BLOCK), lambda b, p: (p, 0))],
        out_specs=pl.BlockSpec((None, seq, LANES), lambda b, p: (b, 0, p)),
        out_shape=jax.ShapeDtypeStruct((batch, seq, D_ATTN), bf16),
        scratch_shapes=[pltpu.VMEM((2, HEADS_PER_STEP, 2 * LANES, Q_BLOCK), bf16),
                        pltpu.VMEM((4, HEADS_PER_STEP, KV_BLOCK, Q_BLOCK), f32),
                        pltpu.VMEM((4, HEADS_PER_STEP, 1, Q_BLOCK), f32),
                        pltpu.VMEM((HEADS_PER_STEP, 1, Q_BLOCK), f32),
                        pltpu.VMEM((HEADS_PER_STEP, HEAD_DIM + ONES_ROWS, Q_BLOCK), f32)],
        compiler_params=pltpu.CompilerParams(
            dimension_semantics=("parallel", "parallel"), vmem_limit_bytes=VMEM_LIMIT_BYTES),
        name="fox_attn",
    )(qt, k, cp, vt, km, cpm, vtm)

    rows = batch * seq
    row = lambda r: (r, 0)
    out = pl.pallas_call(
        _out_ffn_kernel,
        grid=(rows // ROW_BLOCK,),
        in_specs=[pl.BlockSpec((ROW_BLOCK, d_model), row),
                  pl.BlockSpec((ROW_BLOCK, D_ATTN), row),
                  pl.BlockSpec((ROW_BLOCK, D_POOL), row),
                  _const_spec((D_ATTN + D_POOL, d_model), True), _const_spec((1, d_model)),
                  _const_spec((d_model, d_ff), True), _const_spec((d_model, d_ff), True),
                  _const_spec((d_ff, d_model), True)],
        out_specs=pl.BlockSpec((ROW_BLOCK, d_model), row),
        out_shape=jax.ShapeDtypeStruct((rows, d_model), f32),
        compiler_params=pltpu.CompilerParams(
            dimension_semantics=("parallel",), vmem_limit_bytes=VMEM_LIMIT_BYTES),
        name="out_ffn",
    )(x.reshape(rows, d_model), a.reshape(rows, D_ATTN), po.reshape(rows, D_POOL),
      wo, n2w, wg, wu, wd)
    return out.reshape(batch, seq, d_model)
```

```python
import functools

import jax
import jax.numpy as jnp
from jax import lax
from jax.experimental import pallas as pl
from jax.experimental.pallas import tpu as pltpu

N_META = 16
N_HEADS = 8
HEAD_DIM = 64
D_ATTN = N_HEADS * HEAD_DIM
POOL_WINDOWS = (2, 4, 8, 16)
POOL_GROUP_DIM = 128
D_POOL = len(POOL_WINDOWS) * POOL_GROUP_DIM
EPS = 1e-6

LANES = 128
MXU_TILE = 256
HEADS_PER_STEP = LANES // HEAD_DIM
N_SPLIT = 3
GATE_ROWS = 16
ONES_ROWS = 16
MASKED = -1e30
LOG2E = 1.4426950408889634

META_BLOCK = 128
INPROJ_BLOCK = 1024
INPROJ_SUB = 256
ROW_BLOCK = 512
Q_BLOCK = 512
KV_BLOCK = 512
Q_CHUNK = MXU_TILE
LOOP_STAGES = 4
FF_CHUNKS = (1024, 1024, 768)
VMEM_LIMIT_BYTES = 56 * 1024 * 1024


def _f32(x):
    return x.astype(jnp.float32)


def _bf16(x):
    return x.astype(jnp.bfloat16)


def _dot(a, b):
    return jnp.dot(a, b, preferred_element_type=jnp.float32)


def _dot_nt(a, b):
    return lax.dot_general(a, b, (((1,), (1,)), ((), ())), preferred_element_type=jnp.float32)


def _split_bf16(x, n):
    parts = []
    r = x
    for _ in range(n):
        p = _f32(_bf16(r))
        parts.append(p)
        r = r - p
    return parts


def _rms_scale(x):
    return lax.rsqrt(jnp.mean(x * x, axis=-1, keepdims=True) + EPS)


def _tile_lanes(x, n):
    return jnp.concatenate([x] * (n // x.shape[1]), axis=1) if n != x.shape[1] else x


def _norm_project(h, n1w_ref, wt_ref, wku_ref):
    n1 = _bf16(h * _rms_scale(h) * n1w_ref[...])
    proj_t = _dot_nt(wt_ref[...], n1)
    ku = _dot(n1, wku_ref[...])
    return proj_t, ku


def _key_norm(k, kw_ref, bd_ref):
    hi, lo = (_bf16(p) for p in _split_bf16(k * k, 2))
    bd = bd_ref[...]
    halves = []
    for c in range(D_ATTN // MXU_TILE):
        sl = slice(c * MXU_TILE, (c + 1) * MXU_TILE)
        halves.append(_dot(hi[:, sl], bd) + _dot(lo[:, sl], bd))
    ss = jnp.concatenate(halves, axis=1)
    return k * lax.rsqrt(ss * (1.0 / HEAD_DIM) + EPS) * kw_ref[...]


def _gate_prefix(fg_t, bfg_ref, tri_ref, carry, n_valid):
    t = fg_t.shape[1]
    logf = jax.nn.log_sigmoid(fg_t + _tile_lanes(bfg_ref[...], t))
    if n_valid < t:
        lane = lax.broadcasted_iota(jnp.int32, logf.shape, 1)
        logf = jnp.where(lane < n_valid, logf, 0.0)
    parts = _split_bf16(logf, N_SPLIT)
    pad = jnp.zeros((2 * GATE_ROWS - N_SPLIT * N_HEADS, t), jnp.float32)
    stacked = _bf16(jnp.concatenate(parts + [pad], axis=0))
    cs = _dot(stacked, tri_ref[...])
    c = _tile_lanes(carry, t) + cs[0:N_HEADS]
    for p in range(1, N_SPLIT):
        c = c + cs[p * N_HEADS:(p + 1) * N_HEADS]
    total = jnp.sum(logf, axis=1, keepdims=True)
    return c, total


def _key_bias_columns(c, n_valid):
    t = c.shape[1]
    parts = _split_bf16(c * -LOG2E, N_SPLIT)
    if n_valid < t:
        lane = lax.broadcasted_iota(jnp.int32, c.shape, 1)
        parts[0] = jnp.where(lane < n_valid, parts[0], MASKED)
    pad = jnp.zeros((LANES - N_SPLIT * N_HEADS, t), jnp.float32)
    return _bf16(jnp.concatenate(parts + [pad], axis=0).T)


def _meta_inproj_kernel(h_ref, n1w_ref, wt_ref, wku_ref, bfg_ref, kw_ref, bd_ref, tri_ref,
                        k_ref, cp_ref, vt_ref, u_ref, c_ref):
    proj_t, ku = _norm_project(h_ref[...], n1w_ref, wt_ref, wku_ref)
    vt_ref[...] = _bf16(proj_t[D_ATTN:2 * D_ATTN])
    k_ref[...] = _bf16(_key_norm(ku[:, 0:D_ATTN], kw_ref, bd_ref))
    u_ref[...] = ku[0:N_META, D_ATTN:]
    zero = jnp.zeros((N_HEADS, LANES), jnp.float32)
    c, total = _gate_prefix(proj_t[2 * D_ATTN:2 * D_ATTN + N_HEADS], bfg_ref, tri_ref, zero, N_META)
    cp_ref[...] = _key_bias_columns(c, N_META)
    c_ref[...] = jnp.broadcast_to(total, (N_HEADS, LANES))


def _inproj_kernel(x_ref, um_ref, cm_ref, n1w_ref, wt_ref, wku_ref, bfg_ref, qw_ref, kw_ref, bd_ref,
                   tri_ref, wp_ref, ps_ref,
                   qt_ref, k_ref, cp_ref, vt_ref, po_ref,
                   ubuf_ref, carry_ref):
    t = INPROJ_SUB

    @pl.when(pl.program_id(1) == 0)
    def _():
        ubuf_ref[0:N_META, :] = um_ref[...]
        carry_ref[...] = cm_ref[...]

    carry = carry_ref[...]
    n_sub = x_ref.shape[0] // t

    def project(r):
        return _norm_project(x_ref[r * t:(r + 1) * t, :], n1w_ref, wt_ref, wku_ref)

    projected = project(0)
    for r in range(n_sub):
        rows = slice(r * t, (r + 1) * t)
        proj_t, ku = projected
        if r + 1 < n_sub:
            projected = project(r + 1)
        vt_ref[:, rows] = _bf16(proj_t[D_ATTN:2 * D_ATTN])

        q3 = proj_t[0:D_ATTN].reshape(N_HEADS, HEAD_DIM, t)
        q3 = q3 * lax.rsqrt(jnp.mean(q3 * q3, axis=1, keepdims=True) + EPS)
        qt_ref[:, rows] = _bf16(q3.reshape(D_ATTN, t) * _tile_lanes(qw_ref[...], t))

        k_ref[rows, :] = _bf16(_key_norm(ku[:, 0:D_ATTN], kw_ref, bd_ref))

        c, total = _gate_prefix(proj_t[2 * D_ATTN:2 * D_ATTN + N_HEADS], bfg_ref, tri_ref, carry, t)
        cp_ref[rows, :] = _key_bias_columns(c, t)
        carry = carry + total

        base = N_META + r * t
        ubuf_ref[base:base + t, :] = ku[:, D_ATTN:]
        groups = []
        for g, w in enumerate(POOL_WINDOWS):
            lanes = slice(g * POOL_GROUP_DIM, (g + 1) * POOL_GROUP_DIM)
            cur = ubuf_ref[base:base + t, lanes]
            acc = cur
            for j in range(1, w):
                acc = acc + ubuf_ref[base - j:base - j + t, lanes]
            groups.append(acc * (1.0 / w) - cur)
        pooled = _bf16(jnp.concatenate(groups, axis=1))
        halves = []
        for c2 in range(D_POOL // MXU_TILE):
            halves.append(_dot(pooled[:, c2 * MXU_TILE:(c2 + 1) * MXU_TILE], wp_ref[c2]))
        po_ref[rows, :] = _bf16(jnp.concatenate(halves, axis=1) * ps_ref[...])
    carry_ref[...] = carry
    ubuf_ref[0:N_META, :] = ubuf_ref[x_ref.shape[0]:x_ref.shape[0] + N_META, :]


def _attn_kernel(qt_ref, k_ref, cp_ref, vt_ref, km_ref, cpm_ref, vtm_ref, o_ref,
                 qa_ref, s_ref, cmax_ref, m_ref, acc_ref):
    pair = pl.program_id(1)
    bq = Q_BLOCK
    bk = KV_BLOCK
    n_q = o_ref.shape[0] // bq
    heads = range(HEADS_PER_STEP)
    ones = jnp.ones((ONES_ROWS, bk), jnp.bfloat16)
    meta_slot = 2

    def build_q_aug(blk, qbuf):
        qt = qt_ref[:, pl.ds(pl.multiple_of(blk * bq, bq), bq)]
        zeros_q = jnp.zeros((HEAD_DIM, bq), qt.dtype)
        row = lax.broadcasted_iota(jnp.int32, (LANES, bq), 0)
        for hh in heads:
            head = pair * HEADS_PER_STEP + hh
            pieces = [qt[r * HEAD_DIM:(r + 1) * HEAD_DIM] if r == hh else zeros_q for r in heads]
            sel = ((row & (N_HEADS - 1)) == head) & (row < N_SPLIT * N_HEADS)
            qa_ref[qbuf, hh] = jnp.concatenate(
                pieces + [jnp.where(sel, 1.0, 0.0).astype(qt.dtype)], axis=0)

    def scores(k_aug, slot, hh, cols, qbuf, diagonal=False):
        rows = k_aug.shape[0]
        s = _dot(k_aug, qa_ref[qbuf, hh, :, cols])
        if diagonal:
            causal = (lax.broadcasted_iota(jnp.int32, s.shape, 0)
                      <= lax.broadcasted_iota(jnp.int32, s.shape, 1) + cols.start)
            s = jnp.where(causal, s, MASKED)
        s_ref[slot, hh, 0:rows, cols] = s
        cmax_ref[slot, hh, :, cols] = jnp.max(s, axis=0, keepdims=True)

    def update(v_blk, slot, hh, cols):
        rows = v_blk.shape[1]
        m_old = m_ref[hh, :, cols]
        m_new = jnp.maximum(m_old, cmax_ref[slot, hh, :, cols])
        p = _bf16(jnp.exp2(s_ref[slot, hh, 0:rows, cols] - m_new))
        v_aug = jnp.concatenate([v_blk, ones[:, 0:rows]], axis=0)
        acc_ref[hh, :, cols] = jnp.exp2(m_old - m_new) * acc_ref[hh, :, cols] + _dot(v_aug, p)
        m_ref[hh, :, cols] = m_new

    def k_tile(n):
        start = pl.multiple_of(n * bk, bk)
        return jnp.concatenate([k_ref[pl.ds(start, bk), :], cp_ref[pl.ds(start, bk), :]], axis=1)

    def v_tile(n):
        start = pl.multiple_of(n * bk, bk)
        return lambda hh: vt_ref[hh * HEAD_DIM:(hh + 1) * HEAD_DIM, pl.ds(start, bk)]

    def k_meta():
        return jnp.concatenate([km_ref[...], cpm_ref[...]], axis=1)

    def v_meta(hh):
        return vtm_ref[hh * HEAD_DIM:(hh + 1) * HEAD_DIM, :]

    def stage(k_aug, s_slot, v_blk, u_slot, qbuf=None, diagonal=False):
        for hh in heads:
            for c in range(bq // Q_CHUNK):
                cols = slice(c * Q_CHUNK, (c + 1) * Q_CHUNK)
                if k_aug is not None:
                    scores(k_aug, s_slot, hh, cols, qbuf, diagonal)
                if v_blk is not None:
                    update(v_blk(hh), u_slot, hh, cols)

    build_q_aug(0, 0)
    stage(k_tile(0), 0, None, None, 0, diagonal=True)
    stage(k_meta(), meta_slot, None, None, 0)

    def query_block(qi, carry):
        def tile_at(j):
            return jnp.where(j == 0, qi, j - 1)

        for hh in heads:
            m_ref[hh] = jnp.full(m_ref.shape[1:], MASKED, jnp.float32)
            acc_ref[hh] = jnp.zeros(acc_ref.shape[1:], jnp.float32)

        first_dyn = lax.shift_right_logical(qi + 1, 1) & 1
        parity_dyn = qi & 1
        for first in range(2):
            @pl.when(first_dyn == first)
            def _(first=first):
                def run_stages(j0, count):
                    for t in range(count):
                        slots = (1 - first, first) if t % 2 == 0 else (first, 1 - first)
                        tile = tile_at(j0) if t == 0 else j0 + t - 1
                        stage(k_tile(j0 + t), slots[0], v_tile(tile), slots[1], parity_dyn)

                n_long = lax.shift_right_logical(qi, 2)
                lax.fori_loop(0, n_long, lambda i, c: (run_stages(LOOP_STAGES * i, LOOP_STAGES), c)[1], 0)
                lax.fori_loop(0, lax.shift_right_logical(qi, 1) & 1,
                              lambda i, c: (run_stages(LOOP_STAGES * n_long, 2), c)[1], 0)

        next_blk = jnp.minimum(qi + 1, n_q - 1)
        for variant in range(4):
            @pl.when((qi & 3) == variant)
            def _(variant=variant):
                first = ((variant + 1) >> 1) & 1
                parity = variant % 2
                build_q_aug(next_blk, 1 - parity)
                last = first
                if parity:
                    stage(k_tile(qi - 1), 1 - first, v_tile(tile_at(qi - 1)), first, parity)
                    last = 1 - first
                stage(k_tile(next_blk), 1 - last, v_tile(tile_at(qi)), last, 1 - parity,
                      diagonal=True)
                stage(k_meta(), meta_slot + 1 - parity, v_meta, meta_slot + parity, 1 - parity)

        outs = []
        for hh in heads:
            acc = acc_ref[hh]
            outs.append(acc[0:HEAD_DIM] / acc[HEAD_DIM:HEAD_DIM + 1])
        o_ref[pl.ds(pl.multiple_of(qi * bq, bq), bq), :] = _bf16(jnp.concatenate(outs, axis=0).T)
        return carry

    lax.fori_loop(0, n_q, query_block, 0)


def _out_ffn_kernel(x_ref, a_ref, p_ref, wo_ref, n2w_ref, wg_ref, wu_ref, wd_ref, o_ref):
    mix = _dot(jnp.concatenate([a_ref[...], p_ref[...]], axis=1), wo_ref[...])
    h1 = x_ref[...] + mix
    n2 = _bf16(h1 * _rms_scale(h1) * n2w_ref[...])
    acc = h1
    lo = 0
    for width in FF_CHUNKS:
        g = _dot(n2, wg_ref[:, lo:lo + width])
        u = _dot(n2, wu_ref[:, lo:lo + width])
        acc = acc + _dot(_bf16(g * jax.nn.sigmoid(g) * u), wd_ref[lo:lo + width, :])
        lo += width
    o_ref[...] = acc


def _const_spec(shape, single_buffer=False):
    zeros = (0,) * len(shape)
    if single_buffer:
        return pl.BlockSpec(shape, lambda *_: zeros, pipeline_mode=pl.Buffered(1))
    return pl.BlockSpec(shape, lambda *_: zeros)


def _block_diag(blocks):
    n, r, c = blocks.shape
    out = jnp.zeros((n * r, n * c), blocks.dtype)
    for i in range(n):
        out = out.at[i * r:(i + 1) * r, i * c:(i + 1) * c].set(blocks[i])
    return out


def kernel(x, meta_tokens, norm1_w, w_in, b_fgate, q_norm_w, k_norm_w, w_pool, pool_scale, w_out,
           norm2_w, w_gate, w_up, w_down):
    batch, seq, d_model = x.shape
    assert w_in.shape[0] == 1, "one layer"
    assert seq % INPROJ_BLOCK == 0 and INPROJ_BLOCK % INPROJ_SUB == 0
    assert seq % ROW_BLOCK == 0 and seq % Q_BLOCK == 0 and Q_BLOCK == KV_BLOCK
    d_ff = w_gate.shape[-1]
    assert sum(FF_CHUNKS) == d_ff
    f32, bf16 = jnp.float32, jnp.bfloat16

    w = w_in[0]
    w_q, w_k, w_v = (w[:, i * D_ATTN:(i + 1) * D_ATTN] for i in range(3))
    w_fg = w[:, 3 * D_ATTN:3 * D_ATTN + N_HEADS]
    w_u = w[:, 3 * D_ATTN + N_HEADS:]
    wt = jnp.concatenate([w_q.T, w_v.T, w_fg.T, jnp.zeros((GATE_ROWS - N_HEADS, d_model), f32)],
                         axis=0).astype(bf16)
    wku = jnp.concatenate([w_k, w_u], axis=1).astype(bf16)
    n1w = norm1_w[0].reshape(1, d_model)
    n2w = norm2_w[0].reshape(1, d_model)
    bfg = jnp.broadcast_to(b_fgate[0].reshape(N_HEADS, 1), (N_HEADS, LANES))
    qw = jnp.broadcast_to(jnp.tile(q_norm_w[0] * (HEAD_DIM ** -0.5 * LOG2E), N_HEADS).reshape(D_ATTN, 1),
                          (D_ATTN, LANES))
    kw = jnp.tile(k_norm_w[0], N_HEADS).reshape(1, D_ATTN)
    bd = _block_diag(jnp.ones((MXU_TILE // HEAD_DIM, HEAD_DIM, HEAD_DIM), bf16))
    wp = jnp.stack([_block_diag(w_pool[0, 2 * i:2 * i + 2]) for i in range(D_POOL // MXU_TILE)]
                   ).astype(bf16)
    ps = pool_scale[0].reshape(1, D_POOL)
    wo = w_out[0].astype(bf16)
    wg = w_gate[0].astype(bf16)
    wu = w_up[0].astype(bf16)
    wd = w_down[0].astype(bf16)

    def upper_tri(n):
        return (lax.broadcasted_iota(jnp.int32, (n, n), 0)
                <= lax.broadcasted_iota(jnp.int32, (n, n), 1)).astype(bf16)

    meta_blk = jnp.zeros((META_BLOCK, d_model), f32).at[0:N_META].set(meta_tokens)
    n_proj_t = 2 * D_ATTN + GATE_ROWS

    km, cpm, vtm, um, cm = pl.pallas_call(
        _meta_inproj_kernel,
        out_shape=(jax.ShapeDtypeStruct((META_BLOCK, D_ATTN), bf16),
                   jax.ShapeDtypeStruct((META_BLOCK, LANES), bf16),
                   jax.ShapeDtypeStruct((D_ATTN, META_BLOCK), bf16),
                   jax.ShapeDtypeStruct((N_META, D_POOL), f32),
                   jax.ShapeDtypeStruct((N_HEADS, LANES), f32)),
        compiler_params=pltpu.CompilerParams(vmem_limit_bytes=VMEM_LIMIT_BYTES),
        name="meta_inproj",
    )(meta_blk, n1w, wt, wku, bfg, kw, bd, upper_tri(META_BLOCK))

    tok = lambda b, j: (b, j, 0)
    tok_t = lambda b, j: (b, 0, j)
    qt, k, cp, vt, po = pl.pallas_call(
        _inproj_kernel,
        grid=(batch, seq // INPROJ_BLOCK),
        in_specs=[pl.BlockSpec((None, INPROJ_BLOCK, d_model), tok),
                  _const_spec((N_META, D_POOL)), _const_spec((N_HEADS, LANES)),
                  _const_spec((1, d_model)), _const_spec((n_proj_t, d_model)),
                  _const_spec((d_model, D_ATTN + D_POOL)), _const_spec((N_HEADS, LANES)),
                  _const_spec((D_ATTN, LANES)), _const_spec((1, D_ATTN)),
                  _const_spec((MXU_TILE, MXU_TILE)), _const_spec((INPROJ_SUB, INPROJ_SUB)),
                  _const_spec((D_POOL // MXU_TILE, MXU_TILE, MXU_TILE)), _const_spec((1, D_POOL))],
        out_specs=(pl.BlockSpec((None, D_ATTN, INPROJ_BLOCK), tok_t),
                   pl.BlockSpec((None, INPROJ_BLOCK, D_ATTN), tok),
                   pl.BlockSpec((None, INPROJ_BLOCK, LANES), tok),
                   pl.BlockSpec((None, D_ATTN, INPROJ_BLOCK), tok_t),
                   pl.BlockSpec((None, INPROJ_BLOCK, D_POOL), tok)),
        out_shape=(jax.ShapeDtypeStruct((batch, D_ATTN, seq), bf16),
                   jax.ShapeDtypeStruct((batch, seq, D_ATTN), bf16),
                   jax.ShapeDtypeStruct((batch, seq, LANES), bf16),
                   jax.ShapeDtypeStruct((batch, D_ATTN, seq), bf16),
                   jax.ShapeDtypeStruct((batch, seq, D_POOL), bf16)),
        scratch_shapes=[pltpu.VMEM((N_META + INPROJ_BLOCK, D_POOL), f32),
                        pltpu.VMEM((N_HEADS, LANES), f32)],
        compiler_params=pltpu.CompilerParams(
            dimension_semantics=("arbitrary", "arbitrary"), vmem_limit_bytes=VMEM_LIMIT_BYTES),
        name="inproj",
    )(x, um, cm, n1w, wt, wku, bfg, qw, kw, bd, upper_tri(INPROJ_SUB), wp, ps)

    n_pairs = N_HEADS // HEADS_PER_STEP
    a = pl.pallas_call(
        _attn_kernel,
        grid=(batch, n_pairs),
        in_specs=[pl.BlockSpec((None, LANES, seq), lambda b, p: (b, p, 0)),
                  pl.BlockSpec((None, seq, LANES), lambda b, p: (b, 0, p)),
                  pl.BlockSpec((None, seq, LANES), lambda b, p: (b, 0, 0)),
                  pl.BlockSpec((None, LANES, seq), lambda b, p: (b, p, 0)),
                  pl.BlockSpec((META_BLOCK, LANES), lambda b, p: (0, p)),
                  pl.BlockSpec((META_BLOCK, LANES), lambda b, p: (0, 0)),
                  pl.BlockSpec((LANES, META_BLOCK), lambda b, p: (p, 0))],
        out_specs=pl.BlockSpec((None, seq, LANES), lambda b, p: (b, 0, p)),
        out_shape=jax.ShapeDtypeStruct((batch, seq, D_ATTN), bf16),
        scratch_shapes=[pltpu.VMEM((2, HEADS_PER_STEP, 2 * LANES, Q_BLOCK), bf16),
                        pltpu.VMEM((4, HEADS_PER_STEP, KV_BLOCK, Q_BLOCK), f32),
                        pltpu.VMEM((4, HEADS_PER_STEP, 1, Q_BLOCK), f32),
                        pltpu.VMEM((HEADS_PER_STEP, 1, Q_BLOCK), f32),
                        pltpu.VMEM((HEADS_PER_STEP, HEAD_DIM + ONES_ROWS, Q_BLOCK), f32)],
        compiler_params=pltpu.CompilerParams(
            dimension_semantics=("parallel", "parallel"), vmem_limit_bytes=VMEM_LIMIT_BYTES),
        name="fox_attn",
    )(qt, k, cp, vt, km, cpm, vtm)

    rows = batch * seq
    row = lambda r: (r, 0)
    out = pl.pallas_call(
        _out_ffn_kernel,
        grid=(rows // ROW_BLOCK,),
        in_specs=[pl.BlockSpec((ROW_BLOCK, d_model), row),
                  pl.BlockSpec((ROW_BLOCK, D_ATTN), row),
                  pl.BlockSpec((ROW_BLOCK, D_POOL), row),
                  _const_spec((D_ATTN + D_POOL, d_model), True), _const_spec((1, d_model)),
                  _const_spec((d_model, d_ff), True), _const_spec((d_model, d_ff), True),
                  _const_spec((d_ff, d_model), True)],
        out_specs=pl.BlockSpec((ROW_BLOCK, d_model), row),
        out_shape=jax.ShapeDtypeStruct((rows, d_model), f32),
        compiler_params=pltpu.CompilerParams(
            dimension_semantics=("parallel",), vmem_limit_bytes=VMEM_LIMIT_BYTES),
        name="out_ffn",
    )(x.reshape(rows, d_model), a.reshape(rows, D_ATTN), po.reshape(rows, D_POOL),
      wo, n2w, wg, wu, wd)
    return out.reshape(batch, seq, d_model)
```

```python
import functools

import jax
import jax.numpy as jnp
from jax import lax
from jax.experimental import pallas as pl
from jax.experimental.pallas import tpu as pltpu

N_META = 16
N_HEADS = 8
HEAD_DIM = 64
D_ATTN = N_HEADS * HEAD_DIM
POOL_WINDOWS = (2, 4, 8, 16)
POOL_GROUP_DIM = 128
D_POOL = len(POOL_WINDOWS) * POOL_GROUP_DIM
EPS = 1e-6

LANES = 128
MXU_TILE = 256
HEADS_PER_STEP = LANES // HEAD_DIM
N_SPLIT = 3
GATE_ROWS = 16
ONES_ROWS = 16
MASKED = -1e30
LOG2E = 1.4426950408889634

META_BLOCK = 128
INPROJ_BLOCK = 1024
INPROJ_SUB = 256
ROW_BLOCK = 512
Q_BLOCK = 512
KV_BLOCK = 512
Q_CHUNK = MXU_TILE
LOOP_STAGES = 4
FF_CHUNKS = (1024, 1024, 768)
VMEM_LIMIT_BYTES = 56 * 1024 * 1024


def _f32(x):
    return x.astype(jnp.float32)


def _bf16(x):
    return x.astype(jnp.bfloat16)


def _dot(a, b):
    return jnp.dot(a, b, preferred_element_type=jnp.float32)


def _dot_nt(a, b):
    return lax.dot_general(a, b, (((1,), (1,)), ((), ())), preferred_element_type=jnp.float32)


def _split_bf16(x, n):
    parts = []
    r = x
    for _ in range(n):
        p = _f32(_bf16(r))
        parts.append(p)
        r = r - p
    return parts


def _rms_scale(x):
    return lax.rsqrt(jnp.mean(x * x, axis=-1, keepdims=True) + EPS)


def _tile_lanes(x, n):
    return jnp.concatenate([x] * (n // x.shape[1]), axis=1) if n != x.shape[1] else x


def _norm_project(h, n1w_ref, wt_ref, wku_ref):
    n1 = _bf16(h * _rms_scale(h) * n1w_ref[...])
    proj_t = _dot_nt(wt_ref[...], n1)
    ku = _dot(n1, wku_ref[...])
    return proj_t, ku


def _key_norm(k, kw_ref, bd_ref):
    hi, lo = (_bf16(p) for p in _split_bf16(k * k, 2))
    bd = bd_ref[...]
    halves = []
    for c in range(D_ATTN // MXU_TILE):
        sl = slice(c * MXU_TILE, (c + 1) * MXU_TILE)
        halves.append(_dot(hi[:, sl], bd) + _dot(lo[:, sl], bd))
    ss = jnp.concatenate(halves, axis=1)
    return k * lax.rsqrt(ss * (1.0 / HEAD_DIM) + EPS) * kw_ref[...]


def _gate_prefix(fg_t, bfg_ref, tri_ref, carry, n_valid):
    t = fg_t.shape[1]
    logf = jax.nn.log_sigmoid(fg_t + _tile_lanes(bfg_ref[...], t))
    if n_valid < t:
        lane = lax.broadcasted_iota(jnp.int32, logf.shape, 1)
        logf = jnp.where(lane < n_valid, logf, 0.0)
    parts = _split_bf16(logf, N_SPLIT)
    pad = jnp.zeros((2 * GATE_ROWS - N_SPLIT * N_HEADS, t), jnp.float32)
    stacked = _bf16(jnp.concatenate(parts + [pad], axis=0))
    cs = _dot(stacked, tri_ref[...])
    c = _tile_lanes(carry, t) + cs[0:N_HEADS]
    for p in range(1, N_SPLIT):
        c = c + cs[p * N_HEADS:(p + 1) * N_HEADS]
    total = jnp.sum(logf, axis=1, keepdims=True)
    return c, total


def _key_bias_columns(c, n_valid):
    t = c.shape[1]
    parts = _split_bf16(c * -LOG2E, N_SPLIT)
    if n_valid < t:
        lane = lax.broadcasted_iota(jnp.int32, c.shape, 1)
        parts[0] = jnp.where(lane < n_valid, parts[0], MASKED)
    pad = jnp.zeros((LANES - N_SPLIT * N_HEADS, t), jnp.float32)
    return _bf16(jnp.concatenate(parts + [pad], axis=0).T)


def _meta_inproj_kernel(h_ref, n1w_ref, wt_ref, wku_ref, bfg_ref, kw_ref, bd_ref, tri_ref,
                        k_ref, cp_ref, vt_ref, u_ref, c_ref):
    proj_t, ku = _norm_project(h_ref[...], n1w_ref, wt_ref, wku_ref)
    vt_ref[...] = _bf16(proj_t[D_ATTN:2 * D_ATTN])
    k_ref[...] = _bf16(_key_norm(ku[:, 0:D_ATTN], kw_ref, bd_ref))
    u_ref[...] = ku[0:N_META, D_ATTN:]
    zero = jnp.zeros((N_HEADS, LANES), jnp.float32)
    c, total = _gate_prefix(proj_t[2 * D_ATTN:2 * D_ATTN + N_HEADS], bfg_ref, tri_ref, zero, N_META)
    cp_ref[...] = _key_bias_columns(c, N_META)
    c_ref[...] = jnp.broadcast_to(total, (N_HEADS, LANES))


def _inproj_kernel(x_ref, um_ref, cm_ref, n1w_ref, wt_ref, wku_ref, bfg_ref, qw_ref, kw_ref, bd_ref,
                   tri_ref, wp_ref, ps_ref,
                   qt_ref, k_ref, cp_ref, vt_ref, po_ref,
                   ubuf_ref, carry_ref):
    t = INPROJ_SUB

    @pl.when(pl.program_id(1) == 0)
    def _():
        ubuf_ref[0:N_META, :] = um_ref[...]
        carry_ref[...] = cm_ref[...]

    carry = carry_ref[...]
    n_sub = x_ref.shape[0] // t

    def project(r):
        return _norm_project(x_ref[r * t:(r + 1) * t, :], n1w_ref, wt_ref, wku_ref)

    projected = project(0)
    for r in range(n_sub):
        rows = slice(r * t, (r + 1) * t)
        proj_t, ku = projected
        if r + 1 < n_sub:
            projected = project(r + 1)
        vt_ref[:, rows] = _bf16(proj_t[D_ATTN:2 * D_ATTN])

        q3 = proj_t[0:D_ATTN].reshape(N_HEADS, HEAD_DIM, t)
        q3 = q3 * lax.rsqrt(jnp.mean(q3 * q3, axis=1, keepdims=True) + EPS)
        qt_ref[:, rows] = _bf16(q3.reshape(D_ATTN, t) * _tile_lanes(qw_ref[...], t))

        k_ref[rows, :] = _bf16(_key_norm(ku[:, 0:D_ATTN], kw_ref, bd_ref))

        c, total = _gate_prefix(proj_t[2 * D_ATTN:2 * D_ATTN + N_HEADS], bfg_ref, tri_ref, carry, t)
        cp_ref[rows, :] = _key_bias_columns(c, t)
        carry = carry + total

        base = N_META + r * t
        ubuf_ref[base:base + t, :] = ku[:, D_ATTN:]
        groups = []
        for g, w in enumerate(POOL_WINDOWS):
            lanes = slice(g * POOL_GROUP_DIM, (g + 1) * POOL_GROUP_DIM)
            cur = ubuf_ref[base:base + t, lanes]
            acc = cur
            for j in range(1, w):
                acc = acc + ubuf_ref[base - j:base - j + t, lanes]
            groups.append(acc * (1.0 / w) - cur)
        pooled = _bf16(jnp.concatenate(groups, axis=1))
        halves = []
        for c2 in range(D_POOL // MXU_TILE):
            halves.append(_dot(pooled[:, c2 * MXU_TILE:(c2 + 1) * MXU_TILE], wp_ref[c2]))
        po_ref[rows, :] = _bf16(jnp.concatenate(halves, axis=1) * ps_ref[...])
    carry_ref[...] = carry
    ubuf_ref[0:N_META, :] = ubuf_ref[x_ref.shape[0]:x_ref.shape[0] + N_META, :]


def _attn_kernel(qt_ref, k_ref, cp_ref, vt_ref, km_ref, cpm_ref, vtm_ref, o_ref,
                 qa_ref, s_ref, cmax_ref, m_ref, acc_ref):
    pair = pl.program_id(1)
    bq = Q_BLOCK
    bk = KV_BLOCK
    n_q = o_ref.shape[0] // bq
    heads = range(HEADS_PER_STEP)
    ones = jnp.ones((ONES_ROWS, bk), jnp.bfloat16)
    meta_slot = 2

    def init_q_aug():
        row = lax.broadcasted_iota(jnp.int32, (LANES, bq), 0)
        for hh in heads:
            head = pair * HEADS_PER_STEP + hh
            sel = ((row & (N_HEADS - 1)) == head) & (row < N_SPLIT * N_HEADS)
            const = jnp.concatenate([jnp.zeros((LANES, bq), jnp.bfloat16),
                                     jnp.where(sel, 1.0, 0.0).astype(jnp.bfloat16)], axis=0)
            for qbuf in range(2):
                qa_ref[qbuf, hh] = const

    def build_q_aug(blk, qbuf):
        start = pl.multiple_of(blk * bq, bq)
        for hh in heads:
            own = slice(hh * HEAD_DIM, (hh + 1) * HEAD_DIM)
            qa_ref[qbuf, hh, own, :] = qt_ref[own, pl.ds(start, bq)]

    def scores(k_aug, slot, hh, cols, qbuf, diagonal=False):
        rows = k_aug.shape[0]
        s = _dot(k_aug, qa_ref[qbuf, hh, :, cols])
        if diagonal:
            causal = (lax.broadcasted_iota(jnp.int32, s.shape, 0)
                      <= lax.broadcasted_iota(jnp.int32, s.shape, 1) + cols.start)
            s = jnp.where(causal, s, MASKED)
        s_ref[slot, hh, 0:rows, cols] = s
        cmax_ref[slot, hh, :, cols] = jnp.max(s, axis=0, keepdims=True)

    def update(v_blk, slot, hh, cols):
        rows = v_blk.shape[1]
        m_old = m_ref[hh, :, cols]
        m_new = jnp.maximum(m_old, cmax_ref[slot, hh, :, cols])
        p = _bf16(jnp.exp2(s_ref[slot, hh, 0:rows, cols] - m_new))
        v_aug = jnp.concatenate([v_blk, ones[:, 0:rows]], axis=0)
        acc_ref[hh, :, cols] = jnp.exp2(m_old - m_new) * acc_ref[hh, :, cols] + _dot(v_aug, p)
        m_ref[hh, :, cols] = m_new

    def k_tile(n):
        start = pl.multiple_of(n * bk, bk)
        return jnp.concatenate([k_ref[pl.ds(start, bk), :], cp_ref[pl.ds(start, bk), :]], axis=1)

    def v_tile(n):
        start = pl.multiple_of(n * bk, bk)
        return lambda hh: vt_ref[hh * HEAD_DIM:(hh + 1) * HEAD_DIM, pl.ds(start, bk)]

    def k_meta():
        return jnp.concatenate([km_ref[...], cpm_ref[...]], axis=1)

    def v_meta(hh):
        return vtm_ref[hh * HEAD_DIM:(hh + 1) * HEAD_DIM, :]

    def stage(k_aug, s_slot, v_blk, u_slot, qbuf=None, diagonal=False):
        for hh in heads:
            for c in range(bq // Q_CHUNK):
                cols = slice(c * Q_CHUNK, (c + 1) * Q_CHUNK)
                if k_aug is not None:
                    scores(k_aug, s_slot, hh, cols, qbuf, diagonal)
                if v_blk is not None:
                    update(v_blk(hh), u_slot, hh, cols)

    def reset_state():
        for hh in heads:
            m_ref[hh] = jnp.full(m_ref.shape[1:], MASKED, jnp.float32)
            acc_ref[hh] = jnp.zeros(acc_ref.shape[1:], jnp.float32)

    def write_block(qi):
        outs = []
        for hh in heads:
            acc = acc_ref[hh]
            outs.append(acc[0:HEAD_DIM] / acc[HEAD_DIM:HEAD_DIM + 1])
        o_ref[pl.ds(pl.multiple_of(qi * bq, bq), bq), :] = _bf16(jnp.concatenate(outs, axis=0).T)

    init_q_aug()
    build_q_aug(0, 0)
    reset_state()
    stage(k_tile(0), 0, None, None, 0, diagonal=True)
    stage(k_meta(), meta_slot, None, None, 0)

    def query_block(qi, carry):
        def tile_at(j):
            return jnp.where(j == 0, qi, j - 1)

        first_dyn = lax.shift_right_logical(qi + 1, 1) & 1
        parity_dyn = qi & 1
        for first in range(2):
            @pl.when(first_dyn == first)
            def _(first=first):
                def run_stages(j0, count):
                    for t in range(count):
                        slots = (1 - first, first) if t % 2 == 0 else (first, 1 - first)
                        tile = tile_at(j0) if t == 0 else j0 + t - 1
                        stage(k_tile(j0 + t), slots[0], v_tile(tile), slots[1], parity_dyn)

                n_long = lax.shift_right_logical(qi, 2)
                lax.fori_loop(0, n_long, lambda i, c: (run_stages(LOOP_STAGES * i, LOOP_STAGES), c)[1], 0)
                lax.fori_loop(0, lax.shift_right_logical(qi, 1) & 1,
                              lambda i, c: (run_stages(LOOP_STAGES * n_long, 2), c)[1], 0)

        next_blk = jnp.minimum(qi + 1, n_q - 1)
        for variant in range(4):
            @pl.when((qi & 3) == variant)
            def _(variant=variant):
                first = ((variant + 1) >> 1) & 1
                parity = variant % 2
                build_q_aug(next_blk, 1 - parity)
                last = first
                if parity:
                    stage(k_tile(qi - 1), 1 - first, v_tile(tile_at(qi - 1)), first, parity)
                    last = 1 - first
                stage(k_tile(next_blk), 1 - last, v_tile(tile_at(qi)), last, 1 - parity,
                      diagonal=True)
                stage(k_meta(), meta_slot + 1 - parity, v_meta, meta_slot + parity, 1 - parity)
                write_block(qi)
                reset_state()

        return carry

    lax.fori_loop(0, n_q, query_block, 0)


def _out_ffn_kernel(x_ref, a_ref, p_ref, wo_ref, n2w_ref, wg_ref, wu_ref, wd_ref, o_ref):
    mix = _dot(jnp.concatenate([a_ref[...], p_ref[...]], axis=1), wo_ref[...])
    h1 = x_ref[...] + mix
    n2 = _bf16(h1 * _rms_scale(h1) * n2w_ref[...])
    acc = h1
    lo = 0
    for width in FF_CHUNKS:
        g = _dot(n2, wg_ref[:, lo:lo + width])
        u = _dot(n2, wu_ref[:, lo:lo + width])
        acc = acc + _dot(_bf16(g * jax.nn.sigmoid(g) * u), wd_ref[lo:lo + width, :])
        lo += width
    o_ref[...] = acc


def _const_spec(shape, single_buffer=False):
    zeros = (0,) * len(shape)
    if single_buffer:
        return pl.BlockSpec(shape, lambda *_: zeros, pipeline_mode=pl.Buffered(1))
    return pl.BlockSpec(shape, lambda *_: zeros)


def _block_diag(blocks):
    n, r, c = blocks.shape
    out = jnp.zeros((n * r, n * c), blocks.dtype)
    for i in range(n):
        out = out.at[i * r:(i + 1) * r, i * c:(i + 1) * c].set(blocks[i])
    return out


def kernel(x, meta_tokens, norm1_w, w_in, b_fgate, q_norm_w, k_norm_w, w_pool, pool_scale, w_out,
           norm2_w, w_gate, w_up, w_down):
    batch, seq, d_model = x.shape
    assert w_in.shape[0] == 1, "one layer"
    assert seq % INPROJ_BLOCK == 0 and INPROJ_BLOCK % INPROJ_SUB == 0
    assert seq % ROW_BLOCK == 0 and seq % Q_BLOCK == 0 and Q_BLOCK == KV_BLOCK
    d_ff = w_gate.shape[-1]
    assert sum(FF_CHUNKS) == d_ff
    f32, bf16 = jnp.float32, jnp.bfloat16

    w = w_in[0]
    w_q, w_k, w_v = (w[:, i * D_ATTN:(i + 1) * D_ATTN] for i in range(3))
    w_fg = w[:, 3 * D_ATTN:3 * D_ATTN + N_HEADS]
    w_u = w[:, 3 * D_ATTN + N_HEADS:]
    wt = jnp.concatenate([w_q.T, w_v.T, w_fg.T, jnp.zeros((GATE_ROWS - N_HEADS, d_model), f32)],
                         axis=0).astype(bf16)
    wku = jnp.concatenate([w_k, w_u], axis=1).astype(bf16)
    n1w = norm1_w[0].reshape(1, d_model)
    n2w = norm2_w[0].reshape(1, d_model)
    bfg = jnp.broadcast_to(b_fgate[0].reshape(N_HEADS, 1), (N_HEADS, LANES))
    qw = jnp.broadcast_to(jnp.tile(q_norm_w[0] * (HEAD_DIM ** -0.5 * LOG2E), N_HEADS).reshape(D_ATTN, 1),
                          (D_ATTN, LANES))
    kw = jnp.tile(k_norm_w[0], N_HEADS).reshape(1, D_ATTN)
    bd = _block_diag(jnp.ones((MXU_TILE // HEAD_DIM, HEAD_DIM, HEAD_DIM), bf16))
    wp = jnp.stack([_block_diag(w_pool[0, 2 * i:2 * i + 2]) for i in range(D_POOL // MXU_TILE)]
                   ).astype(bf16)
    ps = pool_scale[0].reshape(1, D_POOL)
    wo = w_out[0].astype(bf16)
    wg = w_gate[0].astype(bf16)
    wu = w_up[0].astype(bf16)
    wd = w_down[0].astype(bf16)

    def upper_tri(n):
        return (lax.broadcasted_iota(jnp.int32, (n, n), 0)
                <= lax.broadcasted_iota(jnp.int32, (n, n), 1)).astype(bf16)

    meta_blk = jnp.zeros((META_BLOCK, d_model), f32).at[0:N_META].set(meta_tokens)
    n_proj_t = 2 * D_ATTN + GATE_ROWS

    km, cpm, vtm, um, cm = pl.pallas_call(
        _meta_inproj_kernel,
        out_shape=(jax.ShapeDtypeStruct((META_BLOCK, D_ATTN), bf16),
                   jax.ShapeDtypeStruct((META_BLOCK, LANES), bf16),
                   jax.ShapeDtypeStruct((D_ATTN, META_BLOCK), bf16),
                   jax.ShapeDtypeStruct((N_META, D_POOL), f32),
                   jax.ShapeDtypeStruct((N_HEADS, LANES), f32)),
        compiler_params=pltpu.CompilerParams(vmem_limit_bytes=VMEM_LIMIT_BYTES),
        name="meta_inproj",
    )(meta_blk, n1w, wt, wku, bfg, kw, bd, upper_tri(META_BLOCK))

    tok = lambda b, j: (b, j, 0)
    tok_t = lambda b, j: (b, 0, j)
    qt, k, cp, vt, po = pl.pallas_call(
        _inproj_kernel,
        grid=(batch, seq // INPROJ_BLOCK),
        in_specs=[pl.BlockSpec((None, INPROJ_BLOCK, d_model), tok),
                  _const_spec((N_META, D_POOL)), _const_spec((N_HEADS, LANES)),
                  _const_spec((1, d_model)), _const_spec((n_proj_t, d_model)),
                  _const_spec((d_model, D_ATTN + D_POOL)), _const_spec((N_HEADS, LANES)),
                  _const_spec((D_ATTN, LANES)), _const_spec((1, D_ATTN)),
                  _const_spec((MXU_TILE, MXU_TILE)), _const_spec((INPROJ_SUB, INPROJ_SUB)),
                  _const_spec((D_POOL // MXU_TILE, MXU_TILE, MXU_TILE)), _const_spec((1, D_POOL))],
        out_specs=(pl.BlockSpec((None, D_ATTN, INPROJ_BLOCK), tok_t),
                   pl.BlockSpec((None, INPROJ_BLOCK, D_ATTN), tok),
                   pl.BlockSpec((None, INPROJ_BLOCK, LANES), tok),
                   pl.BlockSpec((None, D_ATTN, INPROJ_BLOCK), tok_t),
                   pl.BlockSpec((None, INPROJ_BLOCK, D_POOL), tok)),
        out_shape=(jax.ShapeDtypeStruct((batch, D_ATTN, seq), bf16),
                   jax.ShapeDtypeStruct((batch, seq, D_ATTN), bf16),
                   jax.ShapeDtypeStruct((batch, seq, LANES), bf16),
                   jax.ShapeDtypeStruct((batch, D_ATTN, seq), bf16),
                   jax.ShapeDtypeStruct((batch, seq, D_POOL), bf16)),
        scratch_shapes=[pltpu.VMEM((N_META + INPROJ_BLOCK, D_POOL), f32),
                        pltpu.VMEM((N_HEADS, LANES), f32)],
        compiler_params=pltpu.CompilerParams(
            dimension_semantics=("arbitrary", "arbitrary"), vmem_limit_bytes=VMEM_LIMIT_BYTES),
        name="inproj",
    )(x, um, cm, n1w, wt, wku, bfg, qw, kw, bd, upper_tri(INPROJ_SUB), wp, ps)

    n_pairs = N_HEADS // HEADS_PER_STEP
    a = pl.pallas_call(
        _attn_kernel,
        grid=(batch, n_pairs),
        in_specs=[pl.BlockSpec((None, LANES, seq), lambda b, p: (b, p, 0)),
                  pl.BlockSpec((None, seq, LANES), lambda b, p: (b, 0, p)),
                  pl.BlockSpec((None, seq, LANES), lambda b, p: (b, 0, 0)),
                  pl.BlockSpec((None, LANES, seq), lambda b, p: (b, p, 0)),
                  pl.BlockSpec((META_BLOCK, LANES), lambda b, p: (0, p)),
                  pl.BlockSpec((META_BLOCK, LANES), lambda b, p: (0, 0)),
                  pl.BlockSpec((LANES, META_BLOCK), lambda b, p: (p, 0))],
        out_specs=pl.BlockSpec((None, seq, LANES), lambda b, p: (b, 0, p)),
        out_shape=jax.ShapeDtypeStruct((batch, seq, D_ATTN), bf16),
        scratch_shapes=[pltpu.VMEM((2, HEADS_PER_STEP, 2 * LANES, Q_BLOCK), bf16),
                        pltpu.VMEM((4, HEADS_PER_STEP, KV_BLOCK, Q_BLOCK), f32),
                        pltpu.VMEM((4, HEADS_PER_STEP, 1, Q_BLOCK), f32),
                        pltpu.VMEM((HEADS_PER_STEP, 1, Q_BLOCK), f32),
                        pltpu.VMEM((HEADS_PER_STEP, HEAD_DIM + ONES_ROWS, Q_BLOCK), f32)],
        compiler_params=pltpu.CompilerParams(
            dimension_semantics=("parallel", "parallel"), vmem_limit_bytes=VMEM_LIMIT_BYTES),
        name="fox_attn",
    )(qt, k, cp, vt, km, cpm, vtm)

    rows = batch * seq
    row = lambda r: (r, 0)
    out = pl.pallas_call(
        _out_ffn_kernel,
        grid=(rows // ROW_BLOCK,),
        in_specs=[pl.BlockSpec((ROW_BLOCK, d_model), row),
                  pl.BlockSpec((ROW_BLOCK, D_ATTN), row),
                  pl.BlockSpec((ROW_BLOCK, D_POOL), row),
                  _const_spec((D_ATTN + D_POOL, d_model), True), _const_spec((1, d_model)),
                  _const_spec((d_model, d_ff), True), _const_spec((d_model, d_ff), True),
                  _const_spec((d_ff, d_model), True)],
        out_specs=pl.BlockSpec((ROW_BLOCK, d_model), row),
        out_shape=jax.ShapeDtypeStruct((rows, d_model), f32),
        compiler_params=pltpu.CompilerParams(
            dimension_semantics=("parallel",), vmem_limit_bytes=VMEM_LIMIT_BYTES),
        name="out_ffn",
    )(x.reshape(rows, d_model), a.reshape(rows, D_ATTN), po.reshape(rows, D_POOL),
      wo, n2w, wg, wu, wd)
    return out.reshape(batch, seq, d_model)
```

```python
import functools

import jax
import jax.numpy as jnp
from jax import lax
from jax.experimental import pallas as pl
from jax.experimental.pallas import tpu as pltpu

N_META = 16
N_HEADS = 8
HEAD_DIM = 64
D_ATTN = N_HEADS * HEAD_DIM
POOL_WINDOWS = (2, 4, 8, 16)
POOL_GROUP_DIM = 128
D_POOL = len(POOL_WINDOWS) * POOL_GROUP_DIM
EPS = 1e-6

LANES = 128
MXU_TILE = 256
HEADS_PER_STEP = LANES // HEAD_DIM
N_SPLIT = 3
GATE_ROWS = 16
ONES_ROWS = 16
MASKED = -1e30
LOG2E = 1.4426950408889634

META_BLOCK = 128
INPROJ_BLOCK = 1024
INPROJ_SUB = 256
ROW_BLOCK = 512
Q_BLOCK = 512
KV_BLOCK = 512
Q_CHUNK = MXU_TILE
LOOP_STAGES = 4
FF_CHUNKS = (1024, 1024, 768)
VMEM_LIMIT_BYTES = 56 * 1024 * 1024


def _f32(x):
    return x.astype(jnp.float32)


def _bf16(x):
    return x.astype(jnp.bfloat16)


def _dot(a, b):
    return jnp.dot(a, b, preferred_element_type=jnp.float32)


def _dot_nt(a, b):
    return lax.dot_general(a, b, (((1,), (1,)), ((), ())), preferred_element_type=jnp.float32)


def _split_bf16(x, n):
    parts = []
    r = x
    for _ in range(n):
        p = _f32(_bf16(r))
        parts.append(p)
        r = r - p
    return parts


def _rms_scale(x):
    return lax.rsqrt(jnp.mean(x * x, axis=-1, keepdims=True) + EPS)


def _tile_lanes(x, n):
    return jnp.concatenate([x] * (n // x.shape[1]), axis=1) if n != x.shape[1] else x


def _norm_project(h, n1w_ref, wt_ref, wku_ref):
    n1 = _bf16(h * _rms_scale(h) * n1w_ref[...])
    proj_t = _dot_nt(wt_ref[...], n1)
    ku = _dot(n1, wku_ref[...])
    return proj_t, ku


def _key_norm(k, kw_ref, bd_ref):
    hi, lo = (_bf16(p) for p in _split_bf16(k * k, 2))
    bd = bd_ref[...]
    halves = []
    for c in range(D_ATTN // MXU_TILE):
        sl = slice(c * MXU_TILE, (c + 1) * MXU_TILE)
        halves.append(_dot(hi[:, sl], bd) + _dot(lo[:, sl], bd))
    ss = jnp.concatenate(halves, axis=1)
    return k * lax.rsqrt(ss * (1.0 / HEAD_DIM) + EPS) * kw_ref[...]


def _gate_prefix(fg_t, bfg_ref, tri_ref, carry, n_valid):
    t = fg_t.shape[1]
    logf = jax.nn.log_sigmoid(fg_t + _tile_lanes(bfg_ref[...], t))
    if n_valid < t:
        lane = lax.broadcasted_iota(jnp.int32, logf.shape, 1)
        logf = jnp.where(lane < n_valid, logf, 0.0)
    parts = _split_bf16(logf, N_SPLIT)
    pad = jnp.zeros((2 * GATE_ROWS - N_SPLIT * N_HEADS, t), jnp.float32)
    stacked = _bf16(jnp.concatenate(parts + [pad], axis=0))
    cs = _dot(stacked, tri_ref[...])
    c = _tile_lanes(carry, t) + cs[0:N_HEADS]
    for p in range(1, N_SPLIT):
        c = c + cs[p * N_HEADS:(p + 1) * N_HEADS]
    total = jnp.sum(logf, axis=1, keepdims=True)
    return c, total


def _key_bias_columns(c, n_valid):
    t = c.shape[1]
    parts = _split_bf16(c * -LOG2E, N_SPLIT)
    if n_valid < t:
        lane = lax.broadcasted_iota(jnp.int32, c.shape, 1)
        parts[0] = jnp.where(lane < n_valid, parts[0], MASKED)
    pad = jnp.zeros((LANES - N_SPLIT * N_HEADS, t), jnp.float32)
    return _bf16(jnp.concatenate(parts + [pad], axis=0).T)


def _meta_inproj_kernel(h_ref, n1w_ref, wt_ref, wku_ref, bfg_ref, kw_ref, bd_ref, tri_ref,
                        k_ref, cp_ref, vt_ref, u_ref, c_ref):
    proj_t, ku = _norm_project(h_ref[...], n1w_ref, wt_ref, wku_ref)
    vt_ref[...] = _bf16(proj_t[D_ATTN:2 * D_ATTN])
    k_ref[...] = _bf16(_key_norm(ku[:, 0:D_ATTN], kw_ref, bd_ref))
    u_ref[...] = ku[0:N_META, D_ATTN:]
    zero = jnp.zeros((N_HEADS, LANES), jnp.float32)
    c, total = _gate_prefix(proj_t[2 * D_ATTN:2 * D_ATTN + N_HEADS], bfg_ref, tri_ref, zero, N_META)
    cp_ref[...] = _key_bias_columns(c, N_META)
    c_ref[...] = jnp.broadcast_to(total, (N_HEADS, LANES))


def _inproj_kernel(x_ref, um_ref, cm_ref, n1w_ref, wt_ref, wku_ref, bfg_ref, qw_ref, kw_ref, bd_ref,
                   tri_ref, wp_ref, ps_ref,
                   qt_ref, k_ref, cp_ref, vt_ref, po_ref,
                   ubuf_ref, carry_ref):
    t = INPROJ_SUB

    @pl.when(pl.program_id(1) == 0)
    def _():
        ubuf_ref[0:N_META, :] = um_ref[...]
        carry_ref[...] = cm_ref[...]

    carry = carry_ref[...]
    n_sub = x_ref.shape[0] // t

    def project(r):
        return _norm_project(x_ref[r * t:(r + 1) * t, :], n1w_ref, wt_ref, wku_ref)

    projected = project(0)
    for r in range(n_sub):
        rows = slice(r * t, (r + 1) * t)
        proj_t, ku = projected
        if r + 1 < n_sub:
            projected = project(r + 1)
        vt_ref[:, rows] = _bf16(proj_t[D_ATTN:2 * D_ATTN])

        q3 = proj_t[0:D_ATTN].reshape(N_HEADS, HEAD_DIM, t)
        q3 = q3 * lax.rsqrt(jnp.mean(q3 * q3, axis=1, keepdims=True) + EPS)
        qt_ref[:, rows] = _bf16(q3.reshape(D_ATTN, t) * _tile_lanes(qw_ref[...], t))

        k_ref[rows, :] = _bf16(_key_norm(ku[:, 0:D_ATTN], kw_ref, bd_ref))

        c, total = _gate_prefix(proj_t[2 * D_ATTN:2 * D_ATTN + N_HEADS], bfg_ref, tri_ref, carry, t)
        cp_ref[rows, :] = _key_bias_columns(c, t)
        carry = carry + total

        base = N_META + r * t
        ubuf_ref[base:base + t, :] = ku[:, D_ATTN:]
        groups = []
        for g, w in enumerate(POOL_WINDOWS):
            lanes = slice(g * POOL_GROUP_DIM, (g + 1) * POOL_GROUP_DIM)
            cur = ubuf_ref[base:base + t, lanes]
            acc = cur
            for j in range(1, w):
                acc = acc + ubuf_ref[base - j:base - j + t, lanes]
            groups.append(acc * (1.0 / w) - cur)
        pooled = _bf16(jnp.concatenate(groups, axis=1))
        halves = []
        for c2 in range(D_POOL // MXU_TILE):
            halves.append(_dot(pooled[:, c2 * MXU_TILE:(c2 + 1) * MXU_TILE], wp_ref[c2]))
        po_ref[rows, :] = _bf16(jnp.concatenate(halves, axis=1) * ps_ref[...])
    carry_ref[...] = carry
    ubuf_ref[0:N_META, :] = ubuf_ref[x_ref.shape[0]:x_ref.shape[0] + N_META, :]


def _attn_kernel(qt_ref, k_ref, cp_ref, vt_ref, km_ref, cpm_ref, vtm_ref, o_ref,
                 qa_ref, s_ref, cmax_ref, m_ref, acc_ref):
    pair = pl.program_id(1)
    bq = Q_BLOCK
    bk = KV_BLOCK
    n_q = o_ref.shape[0] // bq
    heads = range(HEADS_PER_STEP)
    ones = jnp.ones((ONES_ROWS, bk), jnp.bfloat16)
    meta_slot = 2

    def init_q_aug():
        row = lax.broadcasted_iota(jnp.int32, (LANES, bq), 0)
        for hh in heads:
            head = pair * HEADS_PER_STEP + hh
            sel = ((row & (N_HEADS - 1)) == head) & (row < N_SPLIT * N_HEADS)
            const = jnp.concatenate([jnp.zeros((LANES, bq), jnp.bfloat16),
                                     jnp.where(sel, 1.0, 0.0).astype(jnp.bfloat16)], axis=0)
            for qbuf in range(2):
                qa_ref[qbuf, hh] = const

    def build_q_aug(blk, qbuf):
        start = pl.multiple_of(blk * bq, bq)
        for hh in heads:
            own = slice(hh * HEAD_DIM, (hh + 1) * HEAD_DIM)
            qa_ref[qbuf, hh, own, :] = qt_ref[own, pl.ds(start, bq)]

    def scores(k_aug, slot, hh, cols, qbuf, diagonal=False):
        rows = k_aug.shape[0]
        s = _dot(k_aug, qa_ref[qbuf, hh, :, cols])
        if diagonal:
            causal = (lax.broadcasted_iota(jnp.int32, s.shape, 0)
                      <= lax.broadcasted_iota(jnp.int32, s.shape, 1) + cols.start)
            s = jnp.where(causal, s, MASKED)
        s_ref[slot, hh, 0:rows, cols] = s
        cmax_ref[slot, hh, :, cols] = jnp.max(s, axis=0, keepdims=True)

    def update(v_blk, slot, hh, cols, between=None):
        rows = v_blk.shape[1]
        m_old = m_ref[hh, :, cols]
        m_new = jnp.maximum(m_old, cmax_ref[slot, hh, :, cols])
        v_aug = jnp.concatenate([v_blk, ones[:, 0:rows]], axis=0)
        if between is None or rows <= MXU_TILE:
            if between is not None:
                between()
            p = _bf16(jnp.exp2(s_ref[slot, hh, 0:rows, cols] - m_new))
            acc_ref[hh, :, cols] = jnp.exp2(m_old - m_new) * acc_ref[hh, :, cols] + _dot(v_aug, p)
        else:
            half = rows // 2
            p = _bf16(jnp.exp2(s_ref[slot, hh, 0:half, cols] - m_new))
            acc_ref[hh, :, cols] = (jnp.exp2(m_old - m_new) * acc_ref[hh, :, cols]
                                    + _dot(v_aug[:, 0:half], p))
            between()
            p = _bf16(jnp.exp2(s_ref[slot, hh, half:rows, cols] - m_new))
            acc_ref[hh, :, cols] = acc_ref[hh, :, cols] + _dot(v_aug[:, half:rows], p)
        m_ref[hh, :, cols] = m_new

    def k_tile(n):
        start = pl.multiple_of(n * bk, bk)
        return jnp.concatenate([k_ref[pl.ds(start, bk), :], cp_ref[pl.ds(start, bk), :]], axis=1)

    def v_tile(n):
        start = pl.multiple_of(n * bk, bk)
        return lambda hh: vt_ref[hh * HEAD_DIM:(hh + 1) * HEAD_DIM, pl.ds(start, bk)]

    def k_meta():
        return jnp.concatenate([km_ref[...], cpm_ref[...]], axis=1)

    def v_meta(hh):
        return vtm_ref[hh * HEAD_DIM:(hh + 1) * HEAD_DIM, :]

    def stage(k_aug, s_slot, v_blk, u_slot, qbuf=None, diagonal=False):
        for hh in heads:
            for c in range(bq // Q_CHUNK):
                cols = slice(c * Q_CHUNK, (c + 1) * Q_CHUNK)
                if k_aug is not None and v_blk is not None:
                    update(v_blk(hh), u_slot, hh, cols,
                           between=lambda: scores(k_aug, s_slot, hh, cols, qbuf, diagonal))
                elif k_aug is not None:
                    scores(k_aug, s_slot, hh, cols, qbuf, diagonal)
                elif v_blk is not None:
                    update(v_blk(hh), u_slot, hh, cols)

    def reset_state():
        for hh in heads:
            m_ref[hh] = jnp.full(m_ref.shape[1:], MASKED, jnp.float32)
            acc_ref[hh] = jnp.zeros(acc_ref.shape[1:], jnp.float32)

    def write_block(qi):
        outs = []
        for hh in heads:
            acc = acc_ref[hh]
            outs.append(acc[0:HEAD_DIM] / acc[HEAD_DIM:HEAD_DIM + 1])
        o_ref[pl.ds(pl.multiple_of(qi * bq, bq), bq), :] = _bf16(jnp.concatenate(outs, axis=0).T)

    init_q_aug()
    build_q_aug(0, 0)
    reset_state()
    stage(k_tile(0), 0, None, None, 0, diagonal=True)
    stage(k_meta(), meta_slot, None, None, 0)

    def query_block(qi, carry):
        def tile_at(j):
            return jnp.where(j == 0, qi, j - 1)

        first_dyn = lax.shift_right_logical(qi + 1, 1) & 1
        parity_dyn = qi & 1
        for first in range(2):
            @pl.when(first_dyn == first)
            def _(first=first):
                def run_stages(j0, count):
                    for t in range(count):
                        slots = (1 - first, first) if t % 2 == 0 else (first, 1 - first)
                        tile = tile_at(j0) if t == 0 else j0 + t - 1
                        stage(k_tile(j0 + t), slots[0], v_tile(tile), slots[1], parity_dyn)

                n_long = lax.shift_right_logical(qi, 2)
                lax.fori_loop(0, n_long, lambda i, c: (run_stages(LOOP_STAGES * i, LOOP_STAGES), c)[1], 0)
                lax.fori_loop(0, lax.shift_right_logical(qi, 1) & 1,
                              lambda i, c: (run_stages(LOOP_STAGES * n_long, 2), c)[1], 0)

        next_blk = jnp.minimum(qi + 1, n_q - 1)
        for variant in range(4):
            @pl.when((qi & 3) == variant)
            def _(variant=variant):
                first = ((variant + 1) >> 1) & 1
                parity = variant % 2
                build_q_aug(next_blk, 1 - parity)
                last = first
                if parity:
                    stage(k_tile(qi - 1), 1 - first, v_tile(tile_at(qi - 1)), first, parity)
                    last = 1 - first
                stage(k_tile(next_blk), 1 - last, v_tile(tile_at(qi)), last, 1 - parity,
                      diagonal=True)
                stage(k_meta(), meta_slot + 1 - parity, v_meta, meta_slot + parity, 1 - parity)
                write_block(qi)
                reset_state()

        return carry

    lax.fori_loop(0, n_q, query_block, 0)


def _out_ffn_kernel(x_ref, a_ref, p_ref, wo_ref, n2w_ref, wg_ref, wu_ref, wd_ref, o_ref):
    mix = _dot(jnp.concatenate([a_ref[...], p_ref[...]], axis=1), wo_ref[...])
    h1 = x_ref[...] + mix
    n2 = _bf16(h1 * _rms_scale(h1) * n2w_ref[...])
    acc = h1
    lo = 0
    for width in FF_CHUNKS:
        g = _dot(n2, wg_ref[:, lo:lo + width])
        u = _dot(n2, wu_ref[:, lo:lo + width])
        acc = acc + _dot(_bf16(g * jax.nn.sigmoid(g) * u), wd_ref[lo:lo + width, :])
        lo += width
    o_ref[...] = acc


def _const_spec(shape, single_buffer=False):
    zeros = (0,) * len(shape)
    if single_buffer:
        return pl.BlockSpec(shape, lambda *_: zeros, pipeline_mode=pl.Buffered(1))
    return pl.BlockSpec(shape, lambda *_: zeros)


def _block_diag(blocks):
    n, r, c = blocks.shape
    out = jnp.zeros((n * r, n * c), blocks.dtype)
    for i in range(n):
        out = out.at[i * r:(i + 1) * r, i * c:(i + 1) * c].set(blocks[i])
    return out


def kernel(x, meta_tokens, norm1_w, w_in, b_fgate, q_norm_w, k_norm_w, w_pool, pool_scale, w_out,
           norm2_w, w_gate, w_up, w_down):
    batch, seq, d_model = x.shape
    assert w_in.shape[0] == 1, "one layer"
    assert seq % INPROJ_BLOCK == 0 and INPROJ_BLOCK % INPROJ_SUB == 0
    assert seq % ROW_BLOCK == 0 and seq % Q_BLOCK == 0 and Q_BLOCK == KV_BLOCK
    d_ff = w_gate.shape[-1]
    assert sum(FF_CHUNKS) == d_ff
    f32, bf16 = jnp.float32, jnp.bfloat16

    w = w_in[0]
    w_q, w_k, w_v = (w[:, i * D_ATTN:(i + 1) * D_ATTN] for i in range(3))
    w_fg = w[:, 3 * D_ATTN:3 * D_ATTN + N_HEADS]
    w_u = w[:, 3 * D_ATTN + N_HEADS:]
    wt = jnp.concatenate([w_q.T, w_v.T, w_fg.T, jnp.zeros((GATE_ROWS - N_HEADS, d_model), f32)],
                         axis=0).astype(bf16)
    wku = jnp.concatenate([w_k, w_u], axis=1).astype(bf16)
    n1w = norm1_w[0].reshape(1, d_model)
    n2w = norm2_w[0].reshape(1, d_model)
    bfg = jnp.broadcast_to(b_fgate[0].reshape(N_HEADS, 1), (N_HEADS, LANES))
    qw = jnp.broadcast_to(jnp.tile(q_norm_w[0] * (HEAD_DIM ** -0.5 * LOG2E), N_HEADS).reshape(D_ATTN, 1),
                          (D_ATTN, LANES))
    kw = jnp.tile(k_norm_w[0], N_HEADS).reshape(1, D_ATTN)
    bd = _block_diag(jnp.ones((MXU_TILE // HEAD_DIM, HEAD_DIM, HEAD_DIM), bf16))
    wp = jnp.stack([_block_diag(w_pool[0, 2 * i:2 * i + 2]) for i in range(D_POOL // MXU_TILE)]
                   ).astype(bf16)
    ps = pool_scale[0].reshape(1, D_POOL)
    wo = w_out[0].astype(bf16)
    wg = w_gate[0].astype(bf16)
    wu = w_up[0].astype(bf16)
    wd = w_down[0].astype(bf16)

    def upper_tri(n):
        return (lax.broadcasted_iota(jnp.int32, (n, n), 0)
                <= lax.broadcasted_iota(jnp.int32, (n, n), 1)).astype(bf16)

    meta_blk = jnp.zeros((META_BLOCK, d_model), f32).at[0:N_META].set(meta_tokens)
    n_proj_t = 2 * D_ATTN + GATE_ROWS

    km, cpm, vtm, um, cm = pl.pallas_call(
        _meta_inproj_kernel,
        out_shape=(jax.ShapeDtypeStruct((META_BLOCK, D_ATTN), bf16),
                   jax.ShapeDtypeStruct((META_BLOCK, LANES), bf16),
                   jax.ShapeDtypeStruct((D_ATTN, META_BLOCK), bf16),
                   jax.ShapeDtypeStruct((N_META, D_POOL), f32),
                   jax.ShapeDtypeStruct((N_HEADS, LANES), f32)),
        compiler_params=pltpu.CompilerParams(vmem_limit_bytes=VMEM_LIMIT_BYTES),
        name="meta_inproj",
    )(meta_blk, n1w, wt, wku, bfg, kw, bd, upper_tri(META_BLOCK))

    tok = lambda b, j: (b, j, 0)
    tok_t = lambda b, j: (b, 0, j)
    qt, k, cp, vt, po = pl.pallas_call(
        _inproj_kernel,
        grid=(batch, seq // INPROJ_BLOCK),
        in_specs=[pl.BlockSpec((None, INPROJ_BLOCK, d_model), tok),
                  _const_spec((N_META, D_POOL)), _const_spec((N_HEADS, LANES)),
                  _const_spec((1, d_model)), _const_spec((n_proj_t, d_model)),
                  _const_spec((d_model, D_ATTN + D_POOL)), _const_spec((N_HEADS, LANES)),
                  _const_spec((D_ATTN, LANES)), _const_spec((1, D_ATTN)),
                  _const_spec((MXU_TILE, MXU_TILE)), _const_spec((INPROJ_SUB, INPROJ_SUB)),
                  _const_spec((D_POOL // MXU_TILE, MXU_TILE, MXU_TILE)), _const_spec((1, D_POOL))],
        out_specs=(pl.BlockSpec((None, D_ATTN, INPROJ_BLOCK), tok_t),
                   pl.BlockSpec((None, INPROJ_BLOCK, D_ATTN), tok),
                   pl.BlockSpec((None, INPROJ_BLOCK, LANES), tok),
                   pl.BlockSpec((None, D_ATTN, INPROJ_BLOCK), tok_t),
                   pl.BlockSpec((None, INPROJ_BLOCK, D_POOL), tok)),
        out_shape=(jax.ShapeDtypeStruct((batch, D_ATTN, seq), bf16),
                   jax.ShapeDtypeStruct((batch, seq, D_ATTN), bf16),
                   jax.ShapeDtypeStruct((batch, seq, LANES), bf16),
                   jax.ShapeDtypeStruct((batch, D_ATTN, seq), bf16),
                   jax.ShapeDtypeStruct((batch, seq, D_POOL), bf16)),
        scratch_shapes=[pltpu.VMEM((N_META + INPROJ_BLOCK, D_POOL), f32),
                        pltpu.VMEM((N_HEADS, LANES), f32)],
        compiler_params=pltpu.CompilerParams(
            dimension_semantics=("arbitrary", "arbitrary"), vmem_limit_bytes=VMEM_LIMIT_BYTES),
        name="inproj",
    )(x, um, cm, n1w, wt, wku, bfg, qw, kw, bd, upper_tri(INPROJ_SUB), wp, ps)

    n_pairs = N_HEADS // HEADS_PER_STEP
    a = pl.pallas_call(
        _attn_kernel,
        grid=(batch, n_pairs),
        in_specs=[pl.BlockSpec((None, LANES, seq), lambda b, p: (b, p, 0)),
                  pl.BlockSpec((None, seq, LANES), lambda b, p: (b, 0, p)),
                  pl.BlockSpec((None, seq, LANES), lambda b, p: (b, 0, 0)),
                  pl.BlockSpec((None, LANES, seq), lambda b, p: (b, p, 0)),
                  pl.BlockSpec((META_BLOCK, LANES), lambda b, p: (0, p)),
                  pl.BlockSpec((META_BLOCK, LANES), lambda b, p: (0, 0)),
                  pl.BlockSpec((LANES, META_BLOCK), lambda b, p: (p, 0))],
        out_specs=pl.BlockSpec((None, seq, LANES), lambda b, p: (b, 0, p)),
        out_shape=jax.ShapeDtypeStruct((batch, seq, D_ATTN), bf16),
        scratch_shapes=[pltpu.VMEM((2, HEADS_PER_STEP, 2 * LANES, Q_BLOCK), bf16),
                        pltpu.VMEM((4, HEADS_PER_STEP, KV_BLOCK, Q_BLOCK), f32),
                        pltpu.VMEM((4, HEADS_PER_STEP, 1, Q_BLOCK), f32),
                        pltpu.VMEM((HEADS_PER_STEP, 1, Q_BLOCK), f32),
                        pltpu.VMEM((HEADS_PER_STEP, HEAD_DIM + ONES_ROWS, Q_BLOCK), f32)],
        compiler_params=pltpu.CompilerParams(
            dimension_semantics=("parallel", "parallel"), vmem_limit_bytes=VMEM_LIMIT_BYTES),
        name="fox_attn",
    )(qt, k, cp, vt, km, cpm, vtm)

    rows = batch * seq
    row = lambda r: (r, 0)
    out = pl.pallas_call(
        _out_ffn_kernel,
        grid=(rows // ROW_BLOCK,),
        in_specs=[pl.BlockSpec((ROW_BLOCK, d_model), row),
                  pl.BlockSpec((ROW_BLOCK, D_ATTN), row),
                  pl.BlockSpec((ROW_BLOCK, D_POOL), row),
                  _const_spec((D_ATTN + D_POOL, d_model), True), _const_spec((1, d_model)),
                  _const_spec((d_model, d_ff), True), _const_spec((d_model, d_ff), True),
                  _const_spec((d_ff, d_model), True)],
        out_specs=pl.BlockSpec((ROW_BLOCK, d_model), row),
        out_shape=jax.ShapeDtypeStruct((rows, d_model), f32),
        compiler_params=pltpu.CompilerParams(
            dimension_semantics=("parallel",), vmem_limit_bytes=VMEM_LIMIT_BYTES),
        name="out_ffn",
    )(x.reshape(rows, d_model), a.reshape(rows, D_ATTN), po.reshape(rows, D_POOL),
      wo, n2w, wg, wu, wd)
    return out.reshape(batch, seq, d_model)
```

```python
import jax
import jax.numpy as jnp
from jax import lax
from jax.experimental import pallas as pl
from jax.experimental.pallas import tpu as pltpu

N_META = 16
N_HEADS = 8
HEAD_DIM = 64
D_ATTN = N_HEADS * HEAD_DIM
POOL_WINDOWS = (2, 4, 8, 16)
POOL_GROUP_DIM = 128
D_POOL = len(POOL_WINDOWS) * POOL_GROUP_DIM
EPS = 1e-6

LANES = 128
MXU_TILE = 256
HEADS_PER_STEP = LANES // HEAD_DIM
N_SPLIT = 3
GATE_ROWS = 16
ONES_ROWS = 16
MASKED = -1e30
LOG2E = 1.4426950408889634

META_BLOCK = 128
INPROJ_BLOCK = 1024
INPROJ_SUB = 256
ROW_BLOCK = 512
Q_BLOCK = 512
KV_BLOCK = 512
Q_CHUNK = MXU_TILE
LOOP_STAGES = 4
FF_CHUNKS = (1024, 1024, 768)
VMEM_LIMIT_BYTES = 56 * 1024 * 1024


def _f32(x):
    return x.astype(jnp.float32)


def _bf16(x):
    return x.astype(jnp.bfloat16)


def _dot(a, b):
    return jnp.dot(a, b, preferred_element_type=jnp.float32)


def _dot_nt(a, b):
    return lax.dot_general(a, b, (((1,), (1,)), ((), ())), preferred_element_type=jnp.float32)


def _split_bf16(x, n):
    parts = []
    r = x
    for _ in range(n):
        p = _f32(_bf16(r))
        parts.append(p)
        r = r - p
    return parts


def _rms_scale(x):
    return lax.rsqrt(jnp.mean(x * x, axis=-1, keepdims=True) + EPS)


def _tile_lanes(x, n):
    return jnp.concatenate([x] * (n // x.shape[1]), axis=1) if n != x.shape[1] else x


def _norm_project(h, n1w_ref, wt_ref, wku_ref):
    n1 = _bf16(h * _rms_scale(h) * n1w_ref[...])
    proj_t = _dot_nt(wt_ref[...], n1)
    ku = _dot(n1, wku_ref[...])
    return proj_t, ku


def _key_norm(k, kw_ref, bd_ref):
    hi, lo = (_bf16(p) for p in _split_bf16(k * k, 2))
    bd = bd_ref[...]
    halves = []
    for c in range(D_ATTN // MXU_TILE):
        sl = slice(c * MXU_TILE, (c + 1) * MXU_TILE)
        halves.append(_dot(hi[:, sl], bd) + _dot(lo[:, sl], bd))
    ss = jnp.concatenate(halves, axis=1)
    return k * lax.rsqrt(ss * (1.0 / HEAD_DIM) + EPS) * kw_ref[...]


def _gate_prefix(fg_t, bfg_ref, tri_ref, carry, n_valid):
    t = fg_t.shape[1]
    logf = jax.nn.log_sigmoid(fg_t + _tile_lanes(bfg_ref[...], t))
    if n_valid < t:
        lane = lax.broadcasted_iota(jnp.int32, logf.shape, 1)
        logf = jnp.where(lane < n_valid, logf, 0.0)
    parts = _split_bf16(logf, N_SPLIT)
    pad = jnp.zeros((2 * GATE_ROWS - N_SPLIT * N_HEADS, t), jnp.float32)
    stacked = _bf16(jnp.concatenate(parts + [pad], axis=0))
    cs = _dot(stacked, tri_ref[...])
    c = _tile_lanes(carry, t) + cs[0:N_HEADS]
    for p in range(1, N_SPLIT):
        c = c + cs[p * N_HEADS:(p + 1) * N_HEADS]
    total = jnp.sum(logf, axis=1, keepdims=True)
    return c, total


def _key_bias_columns(c, n_valid):
    t = c.shape[1]
    parts = _split_bf16(c * -LOG2E, N_SPLIT)
    if n_valid < t:
        lane = lax.broadcasted_iota(jnp.int32, c.shape, 1)
        parts[0] = jnp.where(lane < n_valid, parts[0], MASKED)
    pad = jnp.zeros((LANES - N_SPLIT * N_HEADS, t), jnp.float32)
    return _bf16(jnp.concatenate(parts + [pad], axis=0).T)


def _meta_inproj_kernel(h_ref, n1w_ref, wt_ref, wku_ref, bfg_ref, kw_ref, bd_ref, tri_ref,
                        k_ref, cp_ref, vt_ref, u_ref, c_ref):
    proj_t, ku = _norm_project(h_ref[...], n1w_ref, wt_ref, wku_ref)
    vt_ref[...] = _bf16(proj_t[D_ATTN:2 * D_ATTN])
    k_ref[...] = _bf16(_key_norm(ku[:, 0:D_ATTN], kw_ref, bd_ref))
    u_ref[...] = ku[0:N_META, D_ATTN:]
    zero = jnp.zeros((N_HEADS, LANES), jnp.float32)
    c, total = _gate_prefix(proj_t[2 * D_ATTN:2 * D_ATTN + N_HEADS], bfg_ref, tri_ref, zero, N_META)
    cp_ref[...] = _key_bias_columns(c, N_META)
    c_ref[...] = jnp.broadcast_to(total, (N_HEADS, LANES))


def _inproj_kernel(x_ref, um_ref, cm_ref, n1w_ref, wt_ref, wku_ref, bfg_ref, qw_ref, kw_ref, bd_ref,
                   tri_ref, wp_ref, ps_ref,
                   qt_ref, k_ref, cp_ref, vt_ref, po_ref,
                   ubuf_ref, carry_ref):
    t = INPROJ_SUB

    @pl.when(pl.program_id(1) == 0)
    def _():
        ubuf_ref[0:N_META, :] = um_ref[...]
        carry_ref[...] = cm_ref[...]

    carry = carry_ref[...]
    n_sub = x_ref.shape[0] // t

    def project(r):
        return _norm_project(x_ref[r * t:(r + 1) * t, :], n1w_ref, wt_ref, wku_ref)

    projected = project(0)
    for r in range(n_sub):
        rows = slice(r * t, (r + 1) * t)
        proj_t, ku = projected
        if r + 1 < n_sub:
            projected = project(r + 1)
        vt_ref[:, rows] = _bf16(proj_t[D_ATTN:2 * D_ATTN])

        q3 = proj_t[0:D_ATTN].reshape(N_HEADS, HEAD_DIM, t)
        q3 = q3 * lax.rsqrt(jnp.mean(q3 * q3, axis=1, keepdims=True) + EPS)
        qt_ref[:, rows] = _bf16(q3.reshape(D_ATTN, t) * _tile_lanes(qw_ref[...], t))

        k_ref[rows, :] = _bf16(_key_norm(ku[:, 0:D_ATTN], kw_ref, bd_ref))

        c, total = _gate_prefix(proj_t[2 * D_ATTN:2 * D_ATTN + N_HEADS], bfg_ref, tri_ref, carry, t)
        cp_ref[rows, :] = _key_bias_columns(c, t)
        carry = carry + total

        base = N_META + r * t
        ubuf_ref[base:base + t, :] = ku[:, D_ATTN:]
        groups = []
        for g, w in enumerate(POOL_WINDOWS):
            lanes = slice(g * POOL_GROUP_DIM, (g + 1) * POOL_GROUP_DIM)
            cur = ubuf_ref[base:base + t, lanes]
            acc = cur
            for j in range(1, w):
                acc = acc + ubuf_ref[base - j:base - j + t, lanes]
            groups.append(acc * (1.0 / w) - cur)
        pooled = _bf16(jnp.concatenate(groups, axis=1))
        halves = []
        for c2 in range(D_POOL // MXU_TILE):
            halves.append(_dot(pooled[:, c2 * MXU_TILE:(c2 + 1) * MXU_TILE], wp_ref[c2]))
        po_ref[rows, :] = _bf16(jnp.concatenate(halves, axis=1) * ps_ref[...])
    carry_ref[...] = carry
    ubuf_ref[0:N_META, :] = ubuf_ref[x_ref.shape[0]:x_ref.shape[0] + N_META, :]


def _attn_kernel(qt_ref, k_ref, cp_ref, vt_ref, km_ref, cpm_ref, vtm_ref,
                 wo_f32_ref, wg_f32_ref, wu_f32_ref, wd_f32_ref,
                 o_ref, wo_ref, wg_ref, wu_ref, wd_ref,
                 qa_ref, s_ref, cmax_ref, m_ref, acc_ref):
    pair = pl.program_id(1)
    bq = Q_BLOCK
    bk = KV_BLOCK
    n_q = o_ref.shape[0] // bq
    heads = range(HEADS_PER_STEP)
    ones = jnp.ones((ONES_ROWS, bk), jnp.bfloat16)
    meta_slot = 2

    def init_q_aug():
        row = lax.broadcasted_iota(jnp.int32, (LANES, bq), 0)
        for hh in heads:
            head = pair * HEADS_PER_STEP + hh
            sel = ((row & (N_HEADS - 1)) == head) & (row < N_SPLIT * N_HEADS)
            const = jnp.concatenate([jnp.zeros((LANES, bq), jnp.bfloat16),
                                     jnp.where(sel, 1.0, 0.0).astype(jnp.bfloat16)], axis=0)
            for qbuf in range(2):
                qa_ref[qbuf, hh] = const

    def build_q_aug(blk, qbuf):
        start = pl.multiple_of(blk * bq, bq)
        for hh in heads:
            own = slice(hh * HEAD_DIM, (hh + 1) * HEAD_DIM)
            qa_ref[qbuf, hh, own, :] = qt_ref[own, pl.ds(start, bq)]

    def scores(k_aug, slot, hh, cols, qbuf, diagonal=False):
        rows = k_aug.shape[0]
        s = _dot(k_aug, qa_ref[qbuf, hh, :, cols])
        if diagonal:
            causal = (lax.broadcasted_iota(jnp.int32, s.shape, 0)
                      <= lax.broadcasted_iota(jnp.int32, s.shape, 1) + cols.start)
            s = jnp.where(causal, s, MASKED)
        s_ref[slot, hh, 0:rows, cols] = s
        cmax_ref[slot, hh, :, cols] = jnp.max(s, axis=0, keepdims=True)

    def update(v_blk, slot, hh, cols, between=None):
        rows = v_blk.shape[1]
        m_old = m_ref[hh, :, cols]
        m_new = jnp.maximum(m_old, cmax_ref[slot, hh, :, cols])
        v_aug = jnp.concatenate([v_blk, ones[:, 0:rows]], axis=0)
        if between is None or rows <= MXU_TILE:
            if between is not None:
                between()
            p = _bf16(jnp.exp2(s_ref[slot, hh, 0:rows, cols] - m_new))
            acc_ref[hh, :, cols] = jnp.exp2(m_old - m_new) * acc_ref[hh, :, cols] + _dot(v_aug, p)
        else:
            half = rows // 2
            p = _bf16(jnp.exp2(s_ref[slot, hh, 0:half, cols] - m_new))
            acc_ref[hh, :, cols] = (jnp.exp2(m_old - m_new) * acc_ref[hh, :, cols]
                                    + _dot(v_aug[:, 0:half], p))
            between()
            p = _bf16(jnp.exp2(s_ref[slot, hh, half:rows, cols] - m_new))
            acc_ref[hh, :, cols] = acc_ref[hh, :, cols] + _dot(v_aug[:, half:rows], p)
        m_ref[hh, :, cols] = m_new

    def k_tile(n):
        start = pl.multiple_of(n * bk, bk)
        return jnp.concatenate([k_ref[pl.ds(start, bk), :], cp_ref[pl.ds(start, bk), :]], axis=1)

    def v_tile(n):
        start = pl.multiple_of(n * bk, bk)
        return lambda hh: vt_ref[hh * HEAD_DIM:(hh + 1) * HEAD_DIM, pl.ds(start, bk)]

    def k_meta():
        return jnp.concatenate([km_ref[...], cpm_ref[...]], axis=1)

    def v_meta(hh):
        return vtm_ref[hh * HEAD_DIM:(hh + 1) * HEAD_DIM, :]

    def stage(k_aug, s_slot, v_blk, u_slot, qbuf=None, diagonal=False):
        for hh in heads:
            for c in range(bq // Q_CHUNK):
                cols = slice(c * Q_CHUNK, (c + 1) * Q_CHUNK)
                if k_aug is not None and v_blk is not None:
                    update(v_blk(hh), u_slot, hh, cols,
                           between=lambda: scores(k_aug, s_slot, hh, cols, qbuf, diagonal))
                elif k_aug is not None:
                    scores(k_aug, s_slot, hh, cols, qbuf, diagonal)
                elif v_blk is not None:
                    update(v_blk(hh), u_slot, hh, cols)

    def reset_state():
        for hh in heads:
            m_ref[hh] = jnp.full(m_ref.shape[1:], MASKED, jnp.float32)
            acc_ref[hh] = jnp.zeros(acc_ref.shape[1:], jnp.float32)

    def write_block(qi):
        outs = []
        for hh in heads:
            acc = acc_ref[hh]
            outs.append(acc[0:HEAD_DIM] / acc[HEAD_DIM:HEAD_DIM + 1])
        o_ref[pl.ds(pl.multiple_of(qi * bq, bq), bq), :] = _bf16(jnp.concatenate(outs, axis=0).T)

    for src, dst in ((wo_f32_ref, wo_ref), (wg_f32_ref, wg_ref), (wu_f32_ref, wu_ref),
                     (wd_f32_ref, wd_ref)):
        dst[...] = _bf16(src[...])

    init_q_aug()
    build_q_aug(0, 0)
    reset_state()
    stage(k_tile(0), 0, None, None, 0, diagonal=True)
    stage(k_meta(), meta_slot, None, None, 0)

    def query_block(qi, carry):
        def tile_at(j):
            return jnp.where(j == 0, qi, j - 1)

        first_dyn = lax.shift_right_logical(qi + 1, 1) & 1
        parity_dyn = qi & 1
        for first in range(2):
            @pl.when(first_dyn == first)
            def _(first=first):
                def run_stages(j0, count):
                    for t in range(count):
                        slots = (1 - first, first) if t % 2 == 0 else (first, 1 - first)
                        tile = tile_at(j0) if t == 0 else j0 + t - 1
                        stage(k_tile(j0 + t), slots[0], v_tile(tile), slots[1], parity_dyn)

                n_long = lax.shift_right_logical(qi, 2)
                lax.fori_loop(0, n_long, lambda i, c: (run_stages(LOOP_STAGES * i, LOOP_STAGES), c)[1], 0)
                lax.fori_loop(0, lax.shift_right_logical(qi, 1) & 1,
                              lambda i, c: (run_stages(LOOP_STAGES * n_long, 2), c)[1], 0)

        next_blk = jnp.minimum(qi + 1, n_q - 1)
        for variant in range(4):
            @pl.when((qi & 3) == variant)
            def _(variant=variant):
                first = ((variant + 1) >> 1) & 1
                parity = variant % 2
                build_q_aug(next_blk, 1 - parity)
                last = first
                if parity:
                    stage(k_tile(qi - 1), 1 - first, v_tile(tile_at(qi - 1)), first, parity)
                    last = 1 - first
                stage(k_tile(next_blk), 1 - last, v_tile(tile_at(qi)), last, 1 - parity,
                      diagonal=True)
                stage(k_meta(), meta_slot + 1 - parity, v_meta, meta_slot + parity, 1 - parity)
                write_block(qi)
                reset_state()

        return carry

    lax.fori_loop(0, n_q, query_block, 0)


def _out_ffn_kernel(x_ref, a_ref, p_ref, wo_ref, n2w_ref, wg_ref, wu_ref, wd_ref, o_ref):
    mix = _dot(jnp.concatenate([a_ref[...], p_ref[...]], axis=1), wo_ref[...])
    h1 = x_ref[...] + mix
    n2 = _bf16(h1 * _rms_scale(h1) * n2w_ref[...])
    acc = h1
    lo = 0
    for width in FF_CHUNKS:
        g = _dot(n2, wg_ref[:, lo:lo + width])
        u = _dot(n2, wu_ref[:, lo:lo + width])
        acc = acc + _dot(_bf16(g * jax.nn.sigmoid(g) * u), wd_ref[lo:lo + width, :])
        lo += width
    o_ref[...] = acc


def _const_spec(shape, single_buffer=False):
    zeros = (0,) * len(shape)
    if single_buffer:
        return pl.BlockSpec(shape, lambda *_: zeros, pipeline_mode=pl.Buffered(1))
    return pl.BlockSpec(shape, lambda *_: zeros)


def _block_diag(blocks):
    n, r, c = blocks.shape
    out = jnp.zeros((n * r, n * c), blocks.dtype)
    for i in range(n):
        out = out.at[i * r:(i + 1) * r, i * c:(i + 1) * c].set(blocks[i])
    return out


def kernel(x, meta_tokens, norm1_w, w_in, b_fgate, q_norm_w, k_norm_w, w_pool, pool_scale, w_out,
           norm2_w, w_gate, w_up, w_down):
    batch, seq, d_model = x.shape
    assert w_in.shape[0] == 1, "one layer"
    assert seq % INPROJ_BLOCK == 0 and INPROJ_BLOCK % INPROJ_SUB == 0
    assert seq % ROW_BLOCK == 0 and seq % Q_BLOCK == 0 and Q_BLOCK == KV_BLOCK
    d_ff = w_gate.shape[-1]
    assert sum(FF_CHUNKS) == d_ff
    f32, bf16 = jnp.float32, jnp.bfloat16

    w = w_in[0]
    w_q, w_k, w_v = (w[:, i * D_ATTN:(i + 1) * D_ATTN] for i in range(3))
    w_fg = w[:, 3 * D_ATTN:3 * D_ATTN + N_HEADS]
    w_u = w[:, 3 * D_ATTN + N_HEADS:]
    wt = jnp.concatenate([w_q.T, w_v.T, w_fg.T, jnp.zeros((GATE_ROWS - N_HEADS, d_model), f32)],
                         axis=0).astype(bf16)
    wku = jnp.concatenate([w_k, w_u], axis=1).astype(bf16)
    n1w = norm1_w[0].reshape(1, d_model)
    n2w = norm2_w[0].reshape(1, d_model)
    bfg = jnp.broadcast_to(b_fgate[0].reshape(N_HEADS, 1), (N_HEADS, LANES))
    qw = jnp.broadcast_to(jnp.tile(q_norm_w[0] * (HEAD_DIM ** -0.5 * LOG2E), N_HEADS).reshape(D_ATTN, 1),
                          (D_ATTN, LANES))
    kw = jnp.tile(k_norm_w[0], N_HEADS).reshape(1, D_ATTN)
    bd = _block_diag(jnp.ones((MXU_TILE // HEAD_DIM, HEAD_DIM, HEAD_DIM), bf16))
    wp = jnp.stack([_block_diag(w_pool[0, 2 * i:2 * i + 2]) for i in range(D_POOL // MXU_TILE)]
                   ).astype(bf16)
    ps = pool_scale[0].reshape(1, D_POOL)

    def upper_tri(n):
        return (lax.broadcasted_iota(jnp.int32, (n, n), 0)
                <= lax.broadcasted_iota(jnp.int32, (n, n), 1)).astype(bf16)

    meta_blk = jnp.zeros((META_BLOCK, d_model), f32).at[0:N_META].set(meta_tokens)
    n_proj_t = 2 * D_ATTN + GATE_ROWS

    km, cpm, vtm, um, cm = pl.pallas_call(
        _meta_inproj_kernel,
        out_shape=(jax.ShapeDtypeStruct((META_BLOCK, D_ATTN), bf16),
                   jax.ShapeDtypeStruct((META_BLOCK, LANES), bf16),
                   jax.ShapeDtypeStruct((D_ATTN, META_BLOCK), bf16),
                   jax.ShapeDtypeStruct((N_META, D_POOL), f32),
                   jax.ShapeDtypeStruct((N_HEADS, LANES), f32)),
        compiler_params=pltpu.CompilerParams(vmem_limit_bytes=VMEM_LIMIT_BYTES),
        name="meta_inproj",
    )(meta_blk, n1w, wt, wku, bfg, kw, bd, upper_tri(META_BLOCK))

    tok = lambda b, j: (b, j, 0)
    tok_t = lambda b, j: (b, 0, j)
    qt, k, cp, vt, po = pl.pallas_call(
        _inproj_kernel,
        grid=(batch, seq // INPROJ_BLOCK),
        in_specs=[pl.BlockSpec((None, INPROJ_BLOCK, d_model), tok),
                  _const_spec((N_META, D_POOL)), _const_spec((N_HEADS, LANES)),
                  _const_spec((1, d_model)), _const_spec((n_proj_t, d_model)),
                  _const_spec((d_model, D_ATTN + D_POOL)), _const_spec((N_HEADS, LANES)),
                  _const_spec((D_ATTN, LANES)), _const_spec((1, D_ATTN)),
                  _const_spec((MXU_TILE, MXU_TILE)), _const_spec((INPROJ_SUB, INPROJ_SUB)),
                  _const_spec((D_POOL // MXU_TILE, MXU_TILE, MXU_TILE)), _const_spec((1, D_POOL))],
        out_specs=(pl.BlockSpec((None, D_ATTN, INPROJ_BLOCK), tok_t),
                   pl.BlockSpec((None, INPROJ_BLOCK, D_ATTN), tok),
                   pl.BlockSpec((None, INPROJ_BLOCK, LANES), tok),
                   pl.BlockSpec((None, D_ATTN, INPROJ_BLOCK), tok_t),
                   pl.BlockSpec((None, INPROJ_BLOCK, D_POOL), tok)),
        out_shape=(jax.ShapeDtypeStruct((batch, D_ATTN, seq), bf16),
                   jax.ShapeDtypeStruct((batch, seq, D_ATTN), bf16),
                   jax.ShapeDtypeStruct((batch, seq, LANES), bf16),
                   jax.ShapeDtypeStruct((batch, D_ATTN, seq), bf16),
                   jax.ShapeDtypeStruct((batch, seq, D_POOL), bf16)),
        scratch_shapes=[pltpu.VMEM((N_META + INPROJ_BLOCK, D_POOL), f32),
                        pltpu.VMEM((N_HEADS, LANES), f32)],
        compiler_params=pltpu.CompilerParams(
            dimension_semantics=("arbitrary", "arbitrary"), vmem_limit_bytes=VMEM_LIMIT_BYTES),
        name="inproj",
    )(x, um, cm, n1w, wt, wku, bfg, qw, kw, bd, upper_tri(INPROJ_SUB), wp, ps)

    n_pairs = N_HEADS // HEADS_PER_STEP
    ffn_weights = (w_out[0], w_gate[0], w_up[0], w_down[0])
    n_steps = batch * n_pairs
    slab = lambda b, p: (b * n_pairs + p, 0)
    slab_specs = []
    for wgt in ffn_weights:
        assert wgt.shape[0] % (n_steps * 16) == 0, "row slabs must be whole bf16 sublane tiles"
        slab_specs.append(pl.BlockSpec((wgt.shape[0] // n_steps, wgt.shape[1]), slab))
    a, wo, wg, wu, wd = pl.pallas_call(
        _attn_kernel,
        grid=(batch, n_pairs),
        in_specs=[pl.BlockSpec((None, LANES, seq), lambda b, p: (b, p, 0)),
                  pl.BlockSpec((None, seq, LANES), lambda b, p: (b, 0, p)),
                  pl.BlockSpec((None, seq, LANES), lambda b, p: (b, 0, 0)),
                  pl.BlockSpec((None, LANES, seq), lambda b, p: (b, p, 0)),
                  pl.BlockSpec((META_BLOCK, LANES), lambda b, p: (0, p)),
                  pl.BlockSpec((META_BLOCK, LANES), lambda b, p: (0, 0)),
                  pl.BlockSpec((LANES, META_BLOCK), lambda b, p: (p, 0))] + slab_specs,
        out_specs=[pl.BlockSpec((None, seq, LANES), lambda b, p: (b, 0, p))] + slab_specs,
        out_shape=[jax.ShapeDtypeStruct((batch, seq, D_ATTN), bf16)]
        + [jax.ShapeDtypeStruct(wgt.shape, bf16) for wgt in ffn_weights],
        scratch_shapes=[pltpu.VMEM((2, HEADS_PER_STEP, 2 * LANES, Q_BLOCK), bf16),
                        pltpu.VMEM((4, HEADS_PER_STEP, KV_BLOCK, Q_BLOCK), f32),
                        pltpu.VMEM((4, HEADS_PER_STEP, 1, Q_BLOCK), f32),
                        pltpu.VMEM((HEADS_PER_STEP, 1, Q_BLOCK), f32),
                        pltpu.VMEM((HEADS_PER_STEP, HEAD_DIM + ONES_ROWS, Q_BLOCK), f32)],
        compiler_params=pltpu.CompilerParams(
            dimension_semantics=("parallel", "parallel"), vmem_limit_bytes=VMEM_LIMIT_BYTES),
        name="fox_attn",
    )(qt, k, cp, vt, km, cpm, vtm, *ffn_weights)

    rows = batch * seq
    row = lambda r: (r, 0)
    out = pl.pallas_call(
        _out_ffn_kernel,
        grid=(rows // ROW_BLOCK,),
        in_specs=[pl.BlockSpec((ROW_BLOCK, d_model), row),
                  pl.BlockSpec((ROW_BLOCK, D_ATTN), row),
                  pl.BlockSpec((ROW_BLOCK, D_POOL), row),
                  _const_spec((D_ATTN + D_POOL, d_model), True), _const_spec((1, d_model)),
                  _const_spec((d_model, d_ff), True), _const_spec((d_model, d_ff), True),
                  _const_spec((d_ff, d_model), True)],
        out_specs=pl.BlockSpec((ROW_BLOCK, d_model), row),
        out_shape=jax.ShapeDtypeStruct((rows, d_model), f32),
        compiler_params=pltpu.CompilerParams(
            dimension_semantics=("parallel",), vmem_limit_bytes=VMEM_LIMIT_BYTES),
        name="out_ffn",
    )(x.reshape(rows, d_model), a.reshape(rows, D_ATTN), po.reshape(rows, D_POOL),
      wo, n2w, wg, wu, wd)
    return out.reshape(batch, seq, d_model)
```

```python
import jax
import jax.numpy as jnp
from jax import lax
from jax.experimental import pallas as pl
from jax.experimental.pallas import tpu as pltpu

N_META = 16
N_HEADS = 8
HEAD_DIM = 64
D_ATTN = N_HEADS * HEAD_DIM
POOL_WINDOWS = (2, 4, 8, 16)
POOL_GROUP_DIM = 128
D_POOL = len(POOL_WINDOWS) * POOL_GROUP_DIM
EPS = 1e-6

LANES = 128
MXU_TILE = 256
HEADS_PER_STEP = LANES // HEAD_DIM
N_SPLIT = 3
GATE_ROWS = 16
ONES_ROWS = 16
MASKED = -1e30
LOG2E = 1.4426950408889634

META_BLOCK = 128
INPROJ_BLOCK = 1024
INPROJ_SUB = 256
ROW_BLOCK = 512
Q_BLOCK = 512
KV_BLOCK = 512
Q_CHUNK = MXU_TILE
LOOP_STAGES = 4
FF_CHUNKS = (1024, 1024, 768)
VMEM_LIMIT_BYTES = 56 * 1024 * 1024


def _f32(x):
    return x.astype(jnp.float32)


def _bf16(x):
    return x.astype(jnp.bfloat16)


def _dot(a, b):
    return jnp.dot(a, b, preferred_element_type=jnp.float32)


def _dot_nt(a, b):
    return lax.dot_general(a, b, (((1,), (1,)), ((), ())), preferred_element_type=jnp.float32)


def _split_bf16(x, n):
    parts = []
    r = x
    for _ in range(n):
        p = _f32(_bf16(r))
        parts.append(p)
        r = r - p
    return parts


def _rms_scale(x):
    return lax.rsqrt(jnp.mean(x * x, axis=-1, keepdims=True) + EPS)


def _tile_lanes(x, n):
    return jnp.concatenate([x] * (n // x.shape[1]), axis=1) if n != x.shape[1] else x


def _norm_project(h, n1w_ref, wt_ref, wku_ref):
    n1 = _bf16(h * _rms_scale(h) * n1w_ref[...])
    proj_t = _dot_nt(wt_ref[...], n1)
    ku = _dot(n1, wku_ref[...])
    return proj_t, ku


def _key_norm(k, kw_ref, bd_ref):
    hi, lo = (_bf16(p) for p in _split_bf16(k * k, 2))
    bd = bd_ref[...]
    halves = []
    for c in range(D_ATTN // MXU_TILE):
        sl = slice(c * MXU_TILE, (c + 1) * MXU_TILE)
        halves.append(_dot(hi[:, sl], bd) + _dot(lo[:, sl], bd))
    ss = jnp.concatenate(halves, axis=1)
    return k * lax.rsqrt(ss * (1.0 / HEAD_DIM) + EPS) * kw_ref[...]


def _gate_prefix(fg_t, bfg_ref, tri_ref, carry, n_valid):
    t = fg_t.shape[1]
    logf = jax.nn.log_sigmoid(fg_t + _tile_lanes(bfg_ref[...], t))
    if n_valid < t:
        lane = lax.broadcasted_iota(jnp.int32, logf.shape, 1)
        logf = jnp.where(lane < n_valid, logf, 0.0)
    parts = _split_bf16(logf, N_SPLIT)
    pad = jnp.zeros((2 * GATE_ROWS - N_SPLIT * N_HEADS, t), jnp.float32)
    stacked = _bf16(jnp.concatenate(parts + [pad], axis=0))
    cs = _dot(stacked, tri_ref[...])
    c = _tile_lanes(carry, t) + cs[0:N_HEADS]
    for p in range(1, N_SPLIT):
        c = c + cs[p * N_HEADS:(p + 1) * N_HEADS]
    total = jnp.sum(logf, axis=1, keepdims=True)
    return c, total


def _key_bias_columns(c, n_valid):
    t = c.shape[1]
    parts = _split_bf16(c * -LOG2E, N_SPLIT)
    if n_valid < t:
        lane = lax.broadcasted_iota(jnp.int32, c.shape, 1)
        parts[0] = jnp.where(lane < n_valid, parts[0], MASKED)
    pad = jnp.zeros((LANES - N_SPLIT * N_HEADS, t), jnp.float32)
    return _bf16(jnp.concatenate(parts + [pad], axis=0).T)


def _meta_inproj_kernel(h_ref, n1w_ref, wt_ref, wku_ref, bfg_ref, kw_ref, bd_ref, tri_ref,
                        k_ref, cp_ref, vt_ref, u_ref, c_ref):
    proj_t, ku = _norm_project(h_ref[...], n1w_ref, wt_ref, wku_ref)
    vt_ref[...] = _bf16(proj_t[D_ATTN:2 * D_ATTN])
    k_ref[...] = _bf16(_key_norm(ku[:, 0:D_ATTN], kw_ref, bd_ref))
    u_ref[...] = ku[0:N_META, D_ATTN:]
    zero = jnp.zeros((N_HEADS, LANES), jnp.float32)
    c, total = _gate_prefix(proj_t[2 * D_ATTN:2 * D_ATTN + N_HEADS], bfg_ref, tri_ref, zero, N_META)
    cp_ref[...] = _key_bias_columns(c, N_META)
    c_ref[...] = jnp.broadcast_to(total, (N_HEADS, LANES))


def _inproj_kernel(x_ref, um_ref, cm_ref, n1w_ref, wt_ref, wku_ref, bfg_ref, qw_ref, kw_ref, bd_ref,
                   tri_ref, wp_ref, ps_ref,
                   qt_ref, k_ref, cp_ref, vt_ref, po_ref,
                   ubuf_ref, carry_ref):
    t = INPROJ_SUB

    @pl.when(pl.program_id(1) == 0)
    def _():
        ubuf_ref[0:N_META, :] = um_ref[...]
        carry_ref[...] = cm_ref[...]

    carry = carry_ref[...]
    n_sub = x_ref.shape[0] // t

    def project(r):
        return _norm_project(x_ref[r * t:(r + 1) * t, :], n1w_ref, wt_ref, wku_ref)

    projected = project(0)
    for r in range(n_sub):
        rows = slice(r * t, (r + 1) * t)
        proj_t, ku = projected
        if r + 1 < n_sub:
            projected = project(r + 1)
        vt_ref[:, rows] = _bf16(proj_t[D_ATTN:2 * D_ATTN])

        q3 = proj_t[0:D_ATTN].reshape(N_HEADS, HEAD_DIM, t)
        q3 = q3 * lax.rsqrt(jnp.mean(q3 * q3, axis=1, keepdims=True) + EPS)
        qt_ref[:, rows] = _bf16(q3.reshape(D_ATTN, t) * _tile_lanes(qw_ref[...], t))

        k_ref[rows, :] = _bf16(_key_norm(ku[:, 0:D_ATTN], kw_ref, bd_ref))

        c, total = _gate_prefix(proj_t[2 * D_ATTN:2 * D_ATTN + N_HEADS], bfg_ref, tri_ref, carry, t)
        cp_ref[rows, :] = _key_bias_columns(c, t)
        carry = carry + total

        base = N_META + r * t
        ubuf_ref[base:base + t, :] = ku[:, D_ATTN:]
        groups = []
        for g, w in enumerate(POOL_WINDOWS):
            lanes = slice(g * POOL_GROUP_DIM, (g + 1) * POOL_GROUP_DIM)
            cur = ubuf_ref[base:base + t, lanes]
            acc = cur
            for j in range(1, w):
                acc = acc + ubuf_ref[base - j:base - j + t, lanes]
            groups.append(acc * (1.0 / w) - cur)
        pooled = _bf16(jnp.concatenate(groups, axis=1))
        halves = []
        for c2 in range(D_POOL // MXU_TILE):
            halves.append(_dot(pooled[:, c2 * MXU_TILE:(c2 + 1) * MXU_TILE], wp_ref[c2]))
        po_ref[rows, :] = _bf16(jnp.concatenate(halves, axis=1) * ps_ref[...])
    carry_ref[...] = carry
    ubuf_ref[0:N_META, :] = ubuf_ref[x_ref.shape[0]:x_ref.shape[0] + N_META, :]


def _attn_kernel(qt_ref, k_ref, cp_ref, vt_ref, km_ref, cpm_ref, vtm_ref,
                 wo_f32_ref, wg_f32_ref, wu_f32_ref, wd_f32_ref,
                 o_ref, wo_ref, wg_ref, wu_ref, wd_ref,
                 qa_ref, s_ref, cmax_ref, m_ref, acc_ref):
    pair = pl.program_id(1)
    bq = Q_BLOCK
    bk = KV_BLOCK
    n_q = o_ref.shape[0] // bq
    heads = range(HEADS_PER_STEP)
    ones = jnp.ones((ONES_ROWS, bk), jnp.bfloat16)
    meta_slot = 2

    def init_q_aug():
        row = lax.broadcasted_iota(jnp.int32, (LANES, bq), 0)
        for hh in heads:
            head = pair * HEADS_PER_STEP + hh
            sel = ((row & (N_HEADS - 1)) == head) & (row < N_SPLIT * N_HEADS)
            const = jnp.concatenate([jnp.zeros((LANES, bq), jnp.bfloat16),
                                     jnp.where(sel, 1.0, 0.0).astype(jnp.bfloat16)], axis=0)
            for qbuf in range(2):
                qa_ref[qbuf, hh] = const

    def build_q_aug(blk, qbuf):
        start = pl.multiple_of(blk * bq, bq)
        for hh in heads:
            own = slice(hh * HEAD_DIM, (hh + 1) * HEAD_DIM)
            qa_ref[qbuf, hh, own, :] = qt_ref[own, pl.ds(start, bq)]

    def scores(k_aug, slot, hh, cols, qbuf, diagonal=False):
        rows = k_aug.shape[0]
        live = min(rows, cols.stop) if diagonal else rows
        s = _dot(k_aug[0:live], qa_ref[qbuf, hh, :, cols])
        if diagonal:
            causal = (lax.broadcasted_iota(jnp.int32, s.shape, 0)
                      <= lax.broadcasted_iota(jnp.int32, s.shape, 1) + cols.start)
            s = jnp.where(causal, s, MASKED)
            if live < rows:
                s_ref[slot, hh, live:rows, cols] = jnp.full((rows - live, cols.stop - cols.start),
                                                            MASKED, jnp.float32)
        s_ref[slot, hh, 0:live, cols] = s
        cmax_ref[slot, hh, :, cols] = jnp.max(s, axis=0, keepdims=True)

    def update(v_blk, slot, hh, cols, between=None):
        rows = v_blk.shape[1]
        m_old = m_ref[hh, :, cols]
        m_new = jnp.maximum(m_old, cmax_ref[slot, hh, :, cols])
        v_aug = jnp.concatenate([v_blk, ones[:, 0:rows]], axis=0)
        if between is None or rows <= MXU_TILE:
            if between is not None:
                between()
            p = _bf16(jnp.exp2(s_ref[slot, hh, 0:rows, cols] - m_new))
            acc_ref[hh, :, cols] = jnp.exp2(m_old - m_new) * acc_ref[hh, :, cols] + _dot(v_aug, p)
        else:
            half = rows // 2
            p = _bf16(jnp.exp2(s_ref[slot, hh, 0:half, cols] - m_new))
            acc_ref[hh, :, cols] = (jnp.exp2(m_old - m_new) * acc_ref[hh, :, cols]
                                    + _dot(v_aug[:, 0:half], p))
            between()
            p = _bf16(jnp.exp2(s_ref[slot, hh, half:rows, cols] - m_new))
            acc_ref[hh, :, cols] = acc_ref[hh, :, cols] + _dot(v_aug[:, half:rows], p)
        m_ref[hh, :, cols] = m_new

    def k_tile(n):
        start = pl.multiple_of(n * bk, bk)
        return jnp.concatenate([k_ref[pl.ds(start, bk), :], cp_ref[pl.ds(start, bk), :]], axis=1)

    def v_tile(n):
        start = pl.multiple_of(n * bk, bk)
        return lambda hh: vt_ref[hh * HEAD_DIM:(hh + 1) * HEAD_DIM, pl.ds(start, bk)]

    def k_meta():
        return jnp.concatenate([km_ref[0:N_META, :], cpm_ref[0:N_META, :]], axis=1)

    def v_meta(hh):
        return vtm_ref[hh * HEAD_DIM:(hh + 1) * HEAD_DIM, 0:N_META]

    def stage(k_aug, s_slot, v_blk, u_slot, qbuf=None, diagonal=False):
        for hh in heads:
            for c in range(bq // Q_CHUNK):
                cols = slice(c * Q_CHUNK, (c + 1) * Q_CHUNK)
                if k_aug is not None and v_blk is not None:
                    update(v_blk(hh), u_slot, hh, cols,
                           between=lambda: scores(k_aug, s_slot, hh, cols, qbuf, diagonal))
                elif k_aug is not None:
                    scores(k_aug, s_slot, hh, cols, qbuf, diagonal)
                elif v_blk is not None:
                    update(v_blk(hh), u_slot, hh, cols)

    def reset_state():
        for hh in heads:
            m_ref[hh] = jnp.full(m_ref.shape[1:], MASKED, jnp.float32)
            acc_ref[hh] = jnp.zeros(acc_ref.shape[1:], jnp.float32)

    def write_block(qi):
        outs = []
        for hh in heads:
            acc = acc_ref[hh]
            outs.append(acc[0:HEAD_DIM] / acc[HEAD_DIM:HEAD_DIM + 1])
        o_ref[pl.ds(pl.multiple_of(qi * bq, bq), bq), :] = _bf16(jnp.concatenate(outs, axis=0).T)

    for src, dst in ((wo_f32_ref, wo_ref), (wg_f32_ref, wg_ref), (wu_f32_ref, wu_ref),
                     (wd_f32_ref, wd_ref)):
        dst[...] = _bf16(src[...])

    init_q_aug()
    build_q_aug(0, 0)
    reset_state()
    stage(k_tile(0), 0, None, None, 0, diagonal=True)
    stage(k_meta(), meta_slot, None, None, 0)

    def query_block(qi, carry):
        def tile_at(j):
            return jnp.where(j == 0, qi, j - 1)

        first_dyn = lax.shift_right_logical(qi + 1, 1) & 1
        parity_dyn = qi & 1
        for first in range(2):
            @pl.when(first_dyn == first)
            def _(first=first):
                def run_stages(j0, count):
                    for t in range(count):
                        slots = (1 - first, first) if t % 2 == 0 else (first, 1 - first)
                        tile = tile_at(j0) if t == 0 else j0 + t - 1
                        stage(k_tile(j0 + t), slots[0], v_tile(tile), slots[1], parity_dyn)

                n_long = lax.shift_right_logical(qi, 2)
                lax.fori_loop(0, n_long, lambda i, c: (run_stages(LOOP_STAGES * i, LOOP_STAGES), c)[1], 0)
                lax.fori_loop(0, lax.shift_right_logical(qi, 1) & 1,
                              lambda i, c: (run_stages(LOOP_STAGES * n_long, 2), c)[1], 0)

        next_blk = jnp.minimum(qi + 1, n_q - 1)
        for variant in range(4):
            @pl.when((qi & 3) == variant)
            def _(variant=variant):
                first = ((variant + 1) >> 1) & 1
                parity = variant % 2
                build_q_aug(next_blk, 1 - parity)
                last = first
                if parity:
                    stage(k_tile(qi - 1), 1 - first, v_tile(tile_at(qi - 1)), first, parity)
                    last = 1 - first
                stage(None, None, v_meta, meta_slot + parity)
                stage(k_tile(next_blk), 1 - last, v_tile(tile_at(qi)), last, 1 - parity,
                      diagonal=True)
                stage(k_meta(), meta_slot + 1 - parity, None, None, 1 - parity)
                write_block(qi)
                reset_state()

        return carry

    lax.fori_loop(0, n_q, query_block, 0)


def _out_ffn_kernel(x_ref, a_ref, p_ref, wo_ref, n2w_ref, wg_ref, wu_ref, wd_ref, o_ref):
    mix = _dot(jnp.concatenate([a_ref[...], p_ref[...]], axis=1), wo_ref[...])
    h1 = x_ref[...] + mix
    n2 = _bf16(h1 * _rms_scale(h1) * n2w_ref[...])
    acc = h1
    lo = 0
    for width in FF_CHUNKS:
        g = _dot(n2, wg_ref[:, lo:lo + width])
        u = _dot(n2, wu_ref[:, lo:lo + width])
        acc = acc + _dot(_bf16(g * jax.nn.sigmoid(g) * u), wd_ref[lo:lo + width, :])
        lo += width
    o_ref[...] = acc


def _const_spec(shape, single_buffer=False):
    zeros = (0,) * len(shape)
    if single_buffer:
        return pl.BlockSpec(shape, lambda *_: zeros, pipeline_mode=pl.Buffered(1))
    return pl.BlockSpec(shape, lambda *_: zeros)


def _block_diag(blocks):
    n, r, c = blocks.shape
    out = jnp.zeros((n * r, n * c), blocks.dtype)
    for i in range(n):
        out = out.at[i * r:(i + 1) * r, i * c:(i + 1) * c].set(blocks[i])
    return out


def kernel(x, meta_tokens, norm1_w, w_in, b_fgate, q_norm_w, k_norm_w, w_pool, pool_scale, w_out,
           norm2_w, w_gate, w_up, w_down):
    batch, seq, d_model = x.shape
    assert w_in.shape[0] == 1, "one layer"
    assert seq % INPROJ_BLOCK == 0 and INPROJ_BLOCK % INPROJ_SUB == 0
    assert seq % ROW_BLOCK == 0 and seq % Q_BLOCK == 0 and Q_BLOCK == KV_BLOCK
    d_ff = w_gate.shape[-1]
    assert sum(FF_CHUNKS) == d_ff
    f32, bf16 = jnp.float32, jnp.bfloat16

    w = w_in[0]
    w_q, w_k, w_v = (w[:, i * D_ATTN:(i + 1) * D_ATTN] for i in range(3))
    w_fg = w[:, 3 * D_ATTN:3 * D_ATTN + N_HEADS]
    w_u = w[:, 3 * D_ATTN + N_HEADS:]
    wt = jnp.concatenate([w_q.T, w_v.T, w_fg.T, jnp.zeros((GATE_ROWS - N_HEADS, d_model), f32)],
                         axis=0).astype(bf16)
    wku = jnp.concatenate([w_k, w_u], axis=1).astype(bf16)
    n1w = norm1_w[0].reshape(1, d_model)
    n2w = norm2_w[0].reshape(1, d_model)
    bfg = jnp.broadcast_to(b_fgate[0].reshape(N_HEADS, 1), (N_HEADS, LANES))
    qw = jnp.broadcast_to(jnp.tile(q_norm_w[0] * (HEAD_DIM ** -0.5 * LOG2E), N_HEADS).reshape(D_ATTN, 1),
                          (D_ATTN, LANES))
    kw = jnp.tile(k_norm_w[0], N_HEADS).reshape(1, D_ATTN)
    bd = _block_diag(jnp.ones((MXU_TILE // HEAD_DIM, HEAD_DIM, HEAD_DIM), bf16))
    wp = jnp.stack([_block_diag(w_pool[0, 2 * i:2 * i + 2]) for i in range(D_POOL // MXU_TILE)]
                   ).astype(bf16)
    ps = pool_scale[0].reshape(1, D_POOL)

    def upper_tri(n):
        return (lax.broadcasted_iota(jnp.int32, (n, n), 0)
                <= lax.broadcasted_iota(jnp.int32, (n, n), 1)).astype(bf16)

    meta_blk = jnp.zeros((META_BLOCK, d_model), f32).at[0:N_META].set(meta_tokens)
    n_proj_t = 2 * D_ATTN + GATE_ROWS

    km, cpm, vtm, um, cm = pl.pallas_call(
        _meta_inproj_kernel,
        out_shape=(jax.ShapeDtypeStruct((META_BLOCK, D_ATTN), bf16),
                   jax.ShapeDtypeStruct((META_BLOCK, LANES), bf16),
                   jax.ShapeDtypeStruct((D_ATTN, META_BLOCK), bf16),
                   jax.ShapeDtypeStruct((N_META, D_POOL), f32),
                   jax.ShapeDtypeStruct((N_HEADS, LANES), f32)),
        compiler_params=pltpu.CompilerParams(vmem_limit_bytes=VMEM_LIMIT_BYTES),
        name="meta_inproj",
    )(meta_blk, n1w, wt, wku, bfg, kw, bd, upper_tri(META_BLOCK))

    tok = lambda b, j: (b, j, 0)
    tok_t = lambda b, j: (b, 0, j)
    qt, k, cp, vt, po = pl.pallas_call(
        _inproj_kernel,
        grid=(batch, seq // INPROJ_BLOCK),
        in_specs=[pl.BlockSpec((None, INPROJ_BLOCK, d_model), tok),
                  _const_spec((N_META, D_POOL)), _const_spec((N_HEADS, LANES)),
                  _const_spec((1, d_model)), _const_spec((n_proj_t, d_model)),
                  _const_spec((d_model, D_ATTN + D_POOL)), _const_spec((N_HEADS, LANES)),
                  _const_spec((D_ATTN, LANES)), _const_spec((1, D_ATTN)),
                  _const_spec((MXU_TILE, MXU_TILE)), _const_spec((INPROJ_SUB, INPROJ_SUB)),
                  _const_spec((D_POOL // MXU_TILE, MXU_TILE, MXU_TILE)), _const_spec((1, D_POOL))],
        out_specs=(pl.BlockSpec((None, D_ATTN, INPROJ_BLOCK), tok_t),
                   pl.BlockSpec((None, INPROJ_BLOCK, D_ATTN), tok),
                   pl.BlockSpec((None, INPROJ_BLOCK, LANES), tok),
                   pl.BlockSpec((None, D_ATTN, INPROJ_BLOCK), tok_t),
                   pl.BlockSpec((None, INPROJ_BLOCK, D_POOL), tok)),
        out_shape=(jax.ShapeDtypeStruct((batch, D_ATTN, seq), bf16),
                   jax.ShapeDtypeStruct((batch, seq, D_ATTN), bf16),
                   jax.ShapeDtypeStruct((batch, seq, LANES), bf16),
                   jax.ShapeDtypeStruct((batch, D_ATTN, seq), bf16),
                   jax.ShapeDtypeStruct((batch, seq, D_POOL), bf16)),
        scratch_shapes=[pltpu.VMEM((N_META + INPROJ_BLOCK, D_POOL), f32),
                        pltpu.VMEM((N_HEADS, LANES), f32)],
        compiler_params=pltpu.CompilerParams(
            dimension_semantics=("arbitrary", "arbitrary"), vmem_limit_bytes=VMEM_LIMIT_BYTES),
        name="inproj",
    )(x, um, cm, n1w, wt, wku, bfg, qw, kw, bd, upper_tri(INPROJ_SUB), wp, ps)

    n_pairs = N_HEADS // HEADS_PER_STEP
    ffn_weights = (w_out[0], w_gate[0], w_up[0], w_down[0])
    n_steps = batch * n_pairs
    slab = lambda b, p: (b * n_pairs + p, 0)
    slab_specs = []
    for wgt in ffn_weights:
        assert wgt.shape[0] % (n_steps * 16) == 0, "row slabs must be whole bf16 sublane tiles"
        slab_specs.append(pl.BlockSpec((wgt.shape[0] // n_steps, wgt.shape[1]), slab))
    a, wo, wg, wu, wd = pl.pallas_call(
        _attn_kernel,
        grid=(batch, n_pairs),
        in_specs=[pl.BlockSpec((None, LANES, seq), lambda b, p: (b, p, 0)),
                  pl.BlockSpec((None, seq, LANES), lambda b, p: (b, 0, p)),
                  pl.BlockSpec((None, seq, LANES), lambda b, p: (b, 0, 0)),
                  pl.BlockSpec((None, LANES, seq), lambda b, p: (b, p, 0)),
                  pl.BlockSpec((META_BLOCK, LANES), lambda b, p: (0, p)),
                  pl.BlockSpec((META_BLOCK, LANES), lambda b, p: (0, 0)),
                  pl.BlockSpec((LANES, META_BLOCK), lambda b, p: (p, 0))] + slab_specs,
        out_specs=[pl.BlockSpec((None, seq, LANES), lambda b, p: (b, 0, p))] + slab_specs,
        out_shape=[jax.ShapeDtypeStruct((batch, seq, D_ATTN), bf16)]
        + [jax.ShapeDtypeStruct(wgt.shape, bf16) for wgt in ffn_weights],
        scratch_shapes=[pltpu.VMEM((2, HEADS_PER_STEP, 2 * LANES, Q_BLOCK), bf16),
                        pltpu.VMEM((4, HEADS_PER_STEP, KV_BLOCK, Q_BLOCK), f32),
                        pltpu.VMEM((4, HEADS_PER_STEP, 1, Q_BLOCK), f32),
                        pltpu.VMEM((HEADS_PER_STEP, 1, Q_BLOCK), f32),
                        pltpu.VMEM((HEADS_PER_STEP, HEAD_DIM + ONES_ROWS, Q_BLOCK), f32)],
        compiler_params=pltpu.CompilerParams(
            dimension_semantics=("parallel", "parallel"), vmem_limit_bytes=VMEM_LIMIT_BYTES),
        name="fox_attn",
    )(qt, k, cp, vt, km, cpm, vtm, *ffn_weights)

    rows = batch * seq
    row = lambda r: (r, 0)
    out = pl.pallas_call(
        _out_ffn_kernel,
        grid=(rows // ROW_BLOCK,),
        in_specs=[pl.BlockSpec((ROW_BLOCK, d_model), row),
                  pl.BlockSpec((ROW_BLOCK, D_ATTN), row),
                  pl.BlockSpec((ROW_BLOCK, D_POOL), row),
                  _const_spec((D_ATTN + D_POOL, d_model), True), _const_spec((1, d_model)),
                  _const_spec((d_model, d_ff), True), _const_spec((d_model, d_ff), True),
                  _const_spec((d_ff, d_model), True)],
        out_specs=pl.BlockSpec((ROW_BLOCK, d_model), row),
        out_shape=jax.ShapeDtypeStruct((rows, d_model), f32),
        compiler_params=pltpu.CompilerParams(
            dimension_semantics=("parallel",), vmem_limit_bytes=VMEM_LIMIT_BYTES),
        name="out_ffn",
    )(x.reshape(rows, d_model), a.reshape(rows, D_ATTN), po.reshape(rows, D_POOL),
      wo, n2w, wg, wu, wd)
    return out.reshape(batch, seq, d_model)
```

```python
import jax
import jax.numpy as jnp
from jax import lax
from jax.experimental import pallas as pl
from jax.experimental.pallas import tpu as pltpu

N_META = 16
N_HEADS = 8
HEAD_DIM = 64
D_ATTN = N_HEADS * HEAD_DIM
POOL_WINDOWS = (2, 4, 8, 16)
POOL_GROUP_DIM = 128
D_POOL = len(POOL_WINDOWS) * POOL_GROUP_DIM
EPS = 1e-6

LANES = 128
MXU_TILE = 256
HEADS_PER_STEP = LANES // HEAD_DIM
N_SPLIT = 3
GATE_ROWS = 16
ONES_ROWS = 16
MASKED = -1e30
LOG2E = 1.4426950408889634

META_BLOCK = 128
INPROJ_BLOCK = 1024
INPROJ_SUB = 256
ROW_BLOCK = 1024
FFN_SUB = 256
Q_BLOCK = 512
KV_BLOCK = 512
Q_CHUNK = MXU_TILE
LOOP_STAGES = 4
FF_CHUNKS = (1024, 1024, 768)
VMEM_LIMIT_BYTES = 56 * 1024 * 1024


def _f32(x):
    return x.astype(jnp.float32)


def _bf16(x):
    return x.astype(jnp.bfloat16)


def _dot(a, b):
    return jnp.dot(a, b, preferred_element_type=jnp.float32)


def _dot_nt(a, b):
    return lax.dot_general(a, b, (((1,), (1,)), ((), ())), preferred_element_type=jnp.float32)


def _split_bf16(x, n):
    parts = []
    r = x
    for _ in range(n):
        p = _f32(_bf16(r))
        parts.append(p)
        r = r - p
    return parts


def _rms_scale(x):
    return lax.rsqrt(jnp.mean(x * x, axis=-1, keepdims=True) + EPS)


def _tile_lanes(x, n):
    return jnp.concatenate([x] * (n // x.shape[1]), axis=1) if n != x.shape[1] else x


def _norm_project(h, n1w_ref, wt_ref, wku_ref):
    n1 = _bf16(h * _rms_scale(h) * n1w_ref[...])
    proj_t = _dot_nt(wt_ref[...], n1)
    ku = _dot(n1, wku_ref[...])
    return proj_t, ku


def _key_norm(k, kw_ref, bd_ref):
    hi, lo = (_bf16(p) for p in _split_bf16(k * k, 2))
    bd = bd_ref[...]
    halves = []
    for c in range(D_ATTN // MXU_TILE):
        sl = slice(c * MXU_TILE, (c + 1) * MXU_TILE)
        halves.append(_dot(hi[:, sl], bd) + _dot(lo[:, sl], bd))
    ss = jnp.concatenate(halves, axis=1)
    return k * lax.rsqrt(ss * (1.0 / HEAD_DIM) + EPS) * kw_ref[...]


def _gate_prefix(fg_t, bfg_ref, tri_ref, carry, n_valid):
    t = fg_t.shape[1]
    logf = jax.nn.log_sigmoid(fg_t + _tile_lanes(bfg_ref[...], t))
    if n_valid < t:
        lane = lax.broadcasted_iota(jnp.int32, logf.shape, 1)
        logf = jnp.where(lane < n_valid, logf, 0.0)
    parts = _split_bf16(logf, N_SPLIT)
    pad = jnp.zeros((2 * GATE_ROWS - N_SPLIT * N_HEADS, t), jnp.float32)
    stacked = _bf16(jnp.concatenate(parts + [pad], axis=0))
    cs = _dot(stacked, tri_ref[...])
    c = _tile_lanes(carry, t) + cs[0:N_HEADS]
    for p in range(1, N_SPLIT):
        c = c + cs[p * N_HEADS:(p + 1) * N_HEADS]
    total = jnp.sum(logf, axis=1, keepdims=True)
    return c, total


def _key_bias_columns(c, n_valid):
    t = c.shape[1]
    parts = _split_bf16(c * -LOG2E, N_SPLIT)
    if n_valid < t:
        lane = lax.broadcasted_iota(jnp.int32, c.shape, 1)
        parts[0] = jnp.where(lane < n_valid, parts[0], MASKED)
    pad = jnp.zeros((LANES - N_SPLIT * N_HEADS, t), jnp.float32)
    return _bf16(jnp.concatenate(parts + [pad], axis=0).T)


def _meta_inproj_kernel(h_ref, n1w_ref, wt_ref, wku_ref, bfg_ref, kw_ref, bd_ref, tri_ref,
                        k_ref, cp_ref, vt_ref, u_ref, c_ref):
    proj_t, ku = _norm_project(h_ref[...], n1w_ref, wt_ref, wku_ref)
    vt_ref[...] = _bf16(proj_t[D_ATTN:2 * D_ATTN])
    k_ref[...] = _bf16(_key_norm(ku[:, 0:D_ATTN], kw_ref, bd_ref))
    u_ref[...] = ku[0:N_META, D_ATTN:]
    zero = jnp.zeros((N_HEADS, LANES), jnp.float32)
    c, total = _gate_prefix(proj_t[2 * D_ATTN:2 * D_ATTN + N_HEADS], bfg_ref, tri_ref, zero, N_META)
    cp_ref[...] = _key_bias_columns(c, N_META)
    c_ref[...] = jnp.broadcast_to(total, (N_HEADS, LANES))


def _inproj_kernel(x_ref, um_ref, cm_ref, n1w_ref, wt_ref, wku_ref, bfg_ref, qw_ref, kw_ref, bd_ref,
                   tri_ref, wp_ref, ps_ref,
                   qt_ref, k_ref, cp_ref, vt_ref, po_ref,
                   ubuf_ref, carry_ref):
    t = INPROJ_SUB

    @pl.when(pl.program_id(1) == 0)
    def _():
        ubuf_ref[0:N_META, :] = um_ref[...]
        carry_ref[...] = cm_ref[...]

    carry = carry_ref[...]
    n_sub = x_ref.shape[0] // t

    def project(r):
        return _norm_project(x_ref[r * t:(r + 1) * t, :], n1w_ref, wt_ref, wku_ref)

    projected = project(0)
    for r in range(n_sub):
        rows = slice(r * t, (r + 1) * t)
        proj_t, ku = projected
        if r + 1 < n_sub:
            projected = project(r + 1)
        vt_ref[:, rows] = _bf16(proj_t[D_ATTN:2 * D_ATTN])

        q3 = proj_t[0:D_ATTN].reshape(N_HEADS, HEAD_DIM, t)
        q3 = q3 * lax.rsqrt(jnp.mean(q3 * q3, axis=1, keepdims=True) + EPS)
        qt_ref[:, rows] = _bf16(q3.reshape(D_ATTN, t) * _tile_lanes(qw_ref[...], t))

        k_ref[rows, :] = _bf16(_key_norm(ku[:, 0:D_ATTN], kw_ref, bd_ref))

        c, total = _gate_prefix(proj_t[2 * D_ATTN:2 * D_ATTN + N_HEADS], bfg_ref, tri_ref, carry, t)
        cp_ref[rows, :] = _key_bias_columns(c, t)
        carry = carry + total

        base = N_META + r * t
        ubuf_ref[base:base + t, :] = ku[:, D_ATTN:]
        groups = []
        for g, w in enumerate(POOL_WINDOWS):
            lanes = slice(g * POOL_GROUP_DIM, (g + 1) * POOL_GROUP_DIM)
            cur = ubuf_ref[base:base + t, lanes]
            acc = cur
            for j in range(1, w):
                acc = acc + ubuf_ref[base - j:base - j + t, lanes]
            groups.append(acc * (1.0 / w) - cur)
        pooled = _bf16(jnp.concatenate(groups, axis=1))
        halves = []
        for c2 in range(D_POOL // MXU_TILE):
            halves.append(_dot(pooled[:, c2 * MXU_TILE:(c2 + 1) * MXU_TILE], wp_ref[c2]))
        po_ref[rows, :] = _bf16(jnp.concatenate(halves, axis=1) * ps_ref[...])
    carry_ref[...] = carry
    ubuf_ref[0:N_META, :] = ubuf_ref[x_ref.shape[0]:x_ref.shape[0] + N_META, :]


def _attn_kernel(qt_ref, k_ref, cp_ref, vt_ref, km_ref, cpm_ref, vtm_ref,
                 wo_f32_ref, wg_f32_ref, wu_f32_ref, wd_f32_ref,
                 o_ref, wo_ref, wg_ref, wu_ref, wd_ref,
                 qa_ref, s_ref, cmax_ref, m_ref, acc_ref):
    pair = pl.program_id(1)
    bq = Q_BLOCK
    bk = KV_BLOCK
    n_q = o_ref.shape[0] // bq
    heads = range(HEADS_PER_STEP)
    ones = jnp.ones((ONES_ROWS, bk), jnp.bfloat16)
    meta_slot = 2

    def init_q_aug():
        row = lax.broadcasted_iota(jnp.int32, (LANES, bq), 0)
        for hh in heads:
            head = pair * HEADS_PER_STEP + hh
            sel = ((row & (N_HEADS - 1)) == head) & (row < N_SPLIT * N_HEADS)
            const = jnp.concatenate([jnp.zeros((LANES, bq), jnp.bfloat16),
                                     jnp.where(sel, 1.0, 0.0).astype(jnp.bfloat16)], axis=0)
            for qbuf in range(2):
                qa_ref[qbuf, hh] = const

    def build_q_aug(blk, qbuf):
        start = pl.multiple_of(blk * bq, bq)
        for hh in heads:
            own = slice(hh * HEAD_DIM, (hh + 1) * HEAD_DIM)
            qa_ref[qbuf, hh, own, :] = qt_ref[own, pl.ds(start, bq)]

    def scores(k_aug, slot, hh, cols, qbuf, diagonal=False):
        rows = k_aug.shape[0]
        live = min(rows, cols.stop) if diagonal else rows
        s = _dot(k_aug[0:live], qa_ref[qbuf, hh, :, cols])
        if diagonal:
            causal = (lax.broadcasted_iota(jnp.int32, s.shape, 0)
                      <= lax.broadcasted_iota(jnp.int32, s.shape, 1) + cols.start)
            s = jnp.where(causal, s, MASKED)
            if live < rows:
                s_ref[slot, hh, live:rows, cols] = jnp.full((rows - live, cols.stop - cols.start),
                                                            MASKED, jnp.float32)
        s_ref[slot, hh, 0:live, cols] = s
        cmax_ref[slot, hh, :, cols] = jnp.max(s, axis=0, keepdims=True)

    def update(v_blk, slot, hh, cols, between=None):
        rows = v_blk.shape[1]
        m_old = m_ref[hh, :, cols]
        m_new = jnp.maximum(m_old, cmax_ref[slot, hh, :, cols])
        v_aug = jnp.concatenate([v_blk, ones[:, 0:rows]], axis=0)
        if between is None or rows <= MXU_TILE:
            if between is not None:
                between()
            p = _bf16(jnp.exp2(s_ref[slot, hh, 0:rows, cols] - m_new))
            acc_ref[hh, :, cols] = jnp.exp2(m_old - m_new) * acc_ref[hh, :, cols] + _dot(v_aug, p)
        else:
            half = rows // 2
            p = _bf16(jnp.exp2(s_ref[slot, hh, 0:half, cols] - m_new))
            acc_ref[hh, :, cols] = (jnp.exp2(m_old - m_new) * acc_ref[hh, :, cols]
                                    + _dot(v_aug[:, 0:half], p))
            between()
            p = _bf16(jnp.exp2(s_ref[slot, hh, half:rows, cols] - m_new))
            acc_ref[hh, :, cols] = acc_ref[hh, :, cols] + _dot(v_aug[:, half:rows], p)
        m_ref[hh, :, cols] = m_new

    def k_tile(n):
        start = pl.multiple_of(n * bk, bk)
        return jnp.concatenate([k_ref[pl.ds(start, bk), :], cp_ref[pl.ds(start, bk), :]], axis=1)

    def v_tile(n):
        start = pl.multiple_of(n * bk, bk)
        return lambda hh: vt_ref[hh * HEAD_DIM:(hh + 1) * HEAD_DIM, pl.ds(start, bk)]

    def k_meta():
        return jnp.concatenate([km_ref[0:N_META, :], cpm_ref[0:N_META, :]], axis=1)

    def v_meta(hh):
        return vtm_ref[hh * HEAD_DIM:(hh + 1) * HEAD_DIM, 0:N_META]

    def stage(k_aug, s_slot, v_blk, u_slot, qbuf=None, diagonal=False):
        for hh in heads:
            for c in range(bq // Q_CHUNK):
                cols = slice(c * Q_CHUNK, (c + 1) * Q_CHUNK)
                if k_aug is not None and v_blk is not None:
                    update(v_blk(hh), u_slot, hh, cols,
                           between=lambda: scores(k_aug, s_slot, hh, cols, qbuf, diagonal))
                elif k_aug is not None:
                    scores(k_aug, s_slot, hh, cols, qbuf, diagonal)
                elif v_blk is not None:
                    update(v_blk(hh), u_slot, hh, cols)

    def reset_state():
        for hh in heads:
            m_ref[hh] = jnp.full(m_ref.shape[1:], MASKED, jnp.float32)
            acc_ref[hh] = jnp.zeros(acc_ref.shape[1:], jnp.float32)

    def write_block(qi):
        outs = []
        for hh in heads:
            acc = acc_ref[hh]
            outs.append(acc[0:HEAD_DIM] / acc[HEAD_DIM:HEAD_DIM + 1])
        o_ref[pl.ds(pl.multiple_of(qi * bq, bq), bq), :] = _bf16(jnp.concatenate(outs, axis=0).T)

    for src, dst in ((wo_f32_ref, wo_ref), (wg_f32_ref, wg_ref), (wu_f32_ref, wu_ref),
                     (wd_f32_ref, wd_ref)):
        dst[...] = _bf16(src[...])

    init_q_aug()
    build_q_aug(0, 0)
    reset_state()
    stage(k_tile(0), 0, None, None, 0, diagonal=True)
    stage(k_meta(), meta_slot, None, None, 0)

    def query_block(qi, carry):
        def tile_at(j):
            return jnp.where(j == 0, qi, j - 1)

        first_dyn = lax.shift_right_logical(qi + 1, 1) & 1
        parity_dyn = qi & 1
        for first in range(2):
            @pl.when(first_dyn == first)
            def _(first=first):
                def run_stages(j0, count):
                    for t in range(count):
                        slots = (1 - first, first) if t % 2 == 0 else (first, 1 - first)
                        tile = tile_at(j0) if t == 0 else j0 + t - 1
                        stage(k_tile(j0 + t), slots[0], v_tile(tile), slots[1], parity_dyn)

                n_long = lax.shift_right_logical(qi, 2)
                lax.fori_loop(0, n_long, lambda i, c: (run_stages(LOOP_STAGES * i, LOOP_STAGES), c)[1], 0)
                lax.fori_loop(0, lax.shift_right_logical(qi, 1) & 1,
                              lambda i, c: (run_stages(LOOP_STAGES * n_long, 2), c)[1], 0)

        next_blk = jnp.minimum(qi + 1, n_q - 1)
        for variant in range(4):
            @pl.when((qi & 3) == variant)
            def _(variant=variant):
                first = ((variant + 1) >> 1) & 1
                parity = variant % 2
                build_q_aug(next_blk, 1 - parity)
                last = first
                if parity:
                    stage(k_tile(qi - 1), 1 - first, v_tile(tile_at(qi - 1)), first, parity)
                    last = 1 - first
                stage(None, None, v_meta, meta_slot + parity)
                stage(k_tile(next_blk), 1 - last, v_tile(tile_at(qi)), last, 1 - parity,
                      diagonal=True)
                stage(k_meta(), meta_slot + 1 - parity, None, None, 1 - parity)
                write_block(qi)
                reset_state()

        return carry

    lax.fori_loop(0, n_q, query_block, 0)


def _out_ffn_kernel(x_ref, a_ref, p_ref, wo_ref, n2w_ref, wg_ref, wu_ref, wd_ref, o_ref):
    t = FFN_SUB
    n_sub = x_ref.shape[0] // t

    def mixed(r):
        rows = slice(r * t, (r + 1) * t)
        mix = _dot(jnp.concatenate([a_ref[rows, :], p_ref[rows, :]], axis=1), wo_ref[...])
        h1 = x_ref[rows, :] + mix
        return h1, _bf16(h1 * _rms_scale(h1) * n2w_ref[...])

    nxt = mixed(0)
    for r in range(n_sub):
        h1, n2 = nxt
        if r + 1 < n_sub:
            nxt = mixed(r + 1)
        acc = h1
        lo = 0
        for width in FF_CHUNKS:
            g = _dot(n2, wg_ref[:, lo:lo + width])
            u = _dot(n2, wu_ref[:, lo:lo + width])
            acc = acc + _dot(_bf16(g * jax.nn.sigmoid(g) * u), wd_ref[lo:lo + width, :])
            lo += width
        o_ref[r * t:(r + 1) * t, :] = acc


def _const_spec(shape, single_buffer=False):
    zeros = (0,) * len(shape)
    if single_buffer:
        return pl.BlockSpec(shape, lambda *_: zeros, pipeline_mode=pl.Buffered(1))
    return pl.BlockSpec(shape, lambda *_: zeros)


def _block_diag(blocks):
    n, r, c = blocks.shape
    out = jnp.zeros((n * r, n * c), blocks.dtype)
    for i in range(n):
        out = out.at[i * r:(i + 1) * r, i * c:(i + 1) * c].set(blocks[i])
    return out


def kernel(x, meta_tokens, norm1_w, w_in, b_fgate, q_norm_w, k_norm_w, w_pool, pool_scale, w_out,
           norm2_w, w_gate, w_up, w_down):
    batch, seq, d_model = x.shape
    assert w_in.shape[0] == 1, "one layer"
    assert seq % INPROJ_BLOCK == 0 and INPROJ_BLOCK % INPROJ_SUB == 0
    assert seq % ROW_BLOCK == 0 and seq % Q_BLOCK == 0 and Q_BLOCK == KV_BLOCK
    d_ff = w_gate.shape[-1]
    assert sum(FF_CHUNKS) == d_ff
    f32, bf16 = jnp.float32, jnp.bfloat16

    w = w_in[0]
    w_q, w_k, w_v = (w[:, i * D_ATTN:(i + 1) * D_ATTN] for i in range(3))
    w_fg = w[:, 3 * D_ATTN:3 * D_ATTN + N_HEADS]
    w_u = w[:, 3 * D_ATTN + N_HEADS:]
    wt = jnp.concatenate([w_q.T, w_v.T, w_fg.T, jnp.zeros((GATE_ROWS - N_HEADS, d_model), f32)],
                         axis=0).astype(bf16)
    wku = jnp.concatenate([w_k, w_u], axis=1).astype(bf16)
    n1w = norm1_w[0].reshape(1, d_model)
    n2w = norm2_w[0].reshape(1, d_model)
    bfg = jnp.broadcast_to(b_fgate[0].reshape(N_HEADS, 1), (N_HEADS, LANES))
    qw = jnp.broadcast_to(jnp.tile(q_norm_w[0] * (HEAD_DIM ** -0.5 * LOG2E), N_HEADS).reshape(D_ATTN, 1),
                          (D_ATTN, LANES))
    kw = jnp.tile(k_norm_w[0], N_HEADS).reshape(1, D_ATTN)
    bd = _block_diag(jnp.ones((MXU_TILE // HEAD_DIM, HEAD_DIM, HEAD_DIM), bf16))
    wp = jnp.stack([_block_diag(w_pool[0, 2 * i:2 * i + 2]) for i in range(D_POOL // MXU_TILE)]
                   ).astype(bf16)
    ps = pool_scale[0].reshape(1, D_POOL)

    def upper_tri(n):
        return (lax.broadcasted_iota(jnp.int32, (n, n), 0)
                <= lax.broadcasted_iota(jnp.int32, (n, n), 1)).astype(bf16)

    meta_blk = jnp.zeros((META_BLOCK, d_model), f32).at[0:N_META].set(meta_tokens)
    n_proj_t = 2 * D_ATTN + GATE_ROWS

    km, cpm, vtm, um, cm = pl.pallas_call(
        _meta_inproj_kernel,
        out_shape=(jax.ShapeDtypeStruct((META_BLOCK, D_ATTN), bf16),
                   jax.ShapeDtypeStruct((META_BLOCK, LANES), bf16),
                   jax.ShapeDtypeStruct((D_ATTN, META_BLOCK), bf16),
                   jax.ShapeDtypeStruct((N_META, D_POOL), f32),
                   jax.ShapeDtypeStruct((N_HEADS, LANES), f32)),
        compiler_params=pltpu.CompilerParams(vmem_limit_bytes=VMEM_LIMIT_BYTES),
        name="meta_inproj",
    )(meta_blk, n1w, wt, wku, bfg, kw, bd, upper_tri(META_BLOCK))

    tok = lambda b, j: (b, j, 0)
    tok_t = lambda b, j: (b, 0, j)
    qt, k, cp, vt, po = pl.pallas_call(
        _inproj_kernel,
        grid=(batch, seq // INPROJ_BLOCK),
        in_specs=[pl.BlockSpec((None, INPROJ_BLOCK, d_model), tok),
                  _const_spec((N_META, D_POOL)), _const_spec((N_HEADS, LANES)),
                  _const_spec((1, d_model)), _const_spec((n_proj_t, d_model)),
                  _const_spec((d_model, D_ATTN + D_POOL)), _const_spec((N_HEADS, LANES)),
                  _const_spec((D_ATTN, LANES)), _const_spec((1, D_ATTN)),
                  _const_spec((MXU_TILE, MXU_TILE)), _const_spec((INPROJ_SUB, INPROJ_SUB)),
                  _const_spec((D_POOL // MXU_TILE, MXU_TILE, MXU_TILE)), _const_spec((1, D_POOL))],
        out_specs=(pl.BlockSpec((None, D_ATTN, INPROJ_BLOCK), tok_t),
                   pl.BlockSpec((None, INPROJ_BLOCK, D_ATTN), tok),
                   pl.BlockSpec((None, INPROJ_BLOCK, LANES), tok),
                   pl.BlockSpec((None, D_ATTN, INPROJ_BLOCK), tok_t),
                   pl.BlockSpec((None, INPROJ_BLOCK, D_POOL), tok)),
        out_shape=(jax.ShapeDtypeStruct((batch, D_ATTN, seq), bf16),
                   jax.ShapeDtypeStruct((batch, seq, D_ATTN), bf16),
                   jax.ShapeDtypeStruct((batch, seq, LANES), bf16),
                   jax.ShapeDtypeStruct((batch, D_ATTN, seq), bf16),
                   jax.ShapeDtypeStruct((batch, seq, D_POOL), bf16)),
        scratch_shapes=[pltpu.VMEM((N_META + INPROJ_BLOCK, D_POOL), f32),
                        pltpu.VMEM((N_HEADS, LANES), f32)],
        compiler_params=pltpu.CompilerParams(
            dimension_semantics=("arbitrary", "arbitrary"), vmem_limit_bytes=VMEM_LIMIT_BYTES),
        name="inproj",
    )(x, um, cm, n1w, wt, wku, bfg, qw, kw, bd, upper_tri(INPROJ_SUB), wp, ps)

    n_pairs = N_HEADS // HEADS_PER_STEP
    ffn_weights = (w_out[0], w_gate[0], w_up[0], w_down[0])
    n_steps = batch * n_pairs
    slab = lambda b, p: (b * n_pairs + p, 0)
    slab_specs = []
    for wgt in ffn_weights:
        assert wgt.shape[0] % (n_steps * 16) == 0, "row slabs must be whole bf16 sublane tiles"
        slab_specs.append(pl.BlockSpec((wgt.shape[0] // n_steps, wgt.shape[1]), slab))
    a, wo, wg, wu, wd = pl.pallas_call(
        _attn_kernel,
        grid=(batch, n_pairs),
        in_specs=[pl.BlockSpec((None, LANES, seq), lambda b, p: (b, p, 0)),
                  pl.BlockSpec((None, seq, LANES), lambda b, p: (b, 0, p)),
                  pl.BlockSpec((None, seq, LANES), lambda b, p: (b, 0, 0)),
                  pl.BlockSpec((None, LANES, seq), lambda b, p: (b, p, 0)),
                  pl.BlockSpec((META_BLOCK, LANES), lambda b, p: (0, p)),
                  pl.BlockSpec((META_BLOCK, LANES), lambda b, p: (0, 0)),
                  pl.BlockSpec((LANES, META_BLOCK), lambda b, p: (p, 0))] + slab_specs,
        out_specs=[pl.BlockSpec((None, seq, LANES), lambda b, p: (b, 0, p))] + slab_specs,
        out_shape=[jax.ShapeDtypeStruct((batch, seq, D_ATTN), bf16)]
        + [jax.ShapeDtypeStruct(wgt.shape, bf16) for wgt in ffn_weights],
        scratch_shapes=[pltpu.VMEM((2, HEADS_PER_STEP, 2 * LANES, Q_BLOCK), bf16),
                        pltpu.VMEM((4, HEADS_PER_STEP, KV_BLOCK, Q_BLOCK), f32),
                        pltpu.VMEM((4, HEADS_PER_STEP, 1, Q_BLOCK), f32),
                        pltpu.VMEM((HEADS_PER_STEP, 1, Q_BLOCK), f32),
                        pltpu.VMEM((HEADS_PER_STEP, HEAD_DIM + ONES_ROWS, Q_BLOCK), f32)],
        compiler_params=pltpu.CompilerParams(
            dimension_semantics=("parallel", "parallel"), vmem_limit_bytes=VMEM_LIMIT_BYTES),
        name="fox_attn",
    )(qt, k, cp, vt, km, cpm, vtm, *ffn_weights)

    rows = batch * seq
    row = lambda r: (r, 0)
    out = pl.pallas_call(
        _out_ffn_kernel,
        grid=(rows // ROW_BLOCK,),
        in_specs=[pl.BlockSpec((ROW_BLOCK, d_model), row),
                  pl.BlockSpec((ROW_BLOCK, D_ATTN), row),
                  pl.BlockSpec((ROW_BLOCK, D_POOL), row),
                  _const_spec((D_ATTN + D_POOL, d_model), True), _const_spec((1, d_model)),
                  _const_spec((d_model, d_ff), True), _const_spec((d_model, d_ff), True),
                  _const_spec((d_ff, d_model), True)],
        out_specs=pl.BlockSpec((ROW_BLOCK, d_model), row),
        out_shape=jax.ShapeDtypeStruct((rows, d_model), f32),
        compiler_params=pltpu.CompilerParams(
            dimension_semantics=("parallel",), vmem_limit_bytes=VMEM_LIMIT_BYTES),
        name="out_ffn",
    )(x.reshape(rows, d_model), a.reshape(rows, D_ATTN), po.reshape(rows, D_POOL),
      wo, n2w, wg, wu, wd)
    return out.reshape(batch, seq, d_model)
```

```python
import numpy as np
import jax
import jax.numpy as jnp
from jax import lax
from jax.experimental import pallas as pl
from jax.experimental.pallas import tpu as pltpu

N_META = 16
N_HEADS = 8
HEAD_DIM = 64
D_ATTN = N_HEADS * HEAD_DIM
POOL_WINDOWS = (2, 4, 8, 16)
POOL_GROUP_DIM = 128
D_POOL = len(POOL_WINDOWS) * POOL_GROUP_DIM
EPS = 1e-6

LANES = 128
MXU_TILE = 256
HEADS_PER_STEP = LANES // HEAD_DIM
N_SPLIT = 3
GATE_ROWS = 16
ONES_ROWS = 16
MASKED = -1e30
LOG2E = 1.4426950408889634

META_BLOCK = 128
INPROJ_BLOCK = 1024
INPROJ_SUB = 256
ROW_BLOCK = 1024
FFN_SUB = 256
Q_BLOCK = 512
KV_BLOCK = 512
Q_CHUNK = MXU_TILE
LOOP_STAGES = 4
FF_CHUNKS = (1024, 1024, 768)
VMEM_LIMIT_BYTES = 56 * 1024 * 1024


def _f32(x):
    return x.astype(jnp.float32)


def _bf16(x):
    return x.astype(jnp.bfloat16)


def _dot(a, b):
    return jnp.dot(a, b, preferred_element_type=jnp.float32)


def _dot_nt(a, b):
    return lax.dot_general(a, b, (((1,), (1,)), ((), ())), preferred_element_type=jnp.float32)


def _split_bf16(x, n):
    parts = []
    r = x
    for _ in range(n):
        p = _f32(_bf16(r))
        parts.append(p)
        r = r - p
    return parts


def _rms_scale(x):
    return lax.rsqrt(jnp.mean(x * x, axis=-1, keepdims=True) + EPS)


def _tile_lanes(x, n):
    return jnp.concatenate([x] * (n // x.shape[1]), axis=1) if n != x.shape[1] else x


def _norm_project(h, n1w_ref, wt_ref, wku_ref):
    n1 = _bf16(h * _rms_scale(h) * n1w_ref[...])
    proj_t = _dot_nt(wt_ref[...], n1)
    ku = _dot(n1, wku_ref[...])
    return proj_t, ku


def _key_norm(k, kw_ref, bd_ref):
    hi, lo = (_bf16(p) for p in _split_bf16(k * k, 2))
    bd = bd_ref[...]
    halves = []
    for c in range(D_ATTN // MXU_TILE):
        sl = slice(c * MXU_TILE, (c + 1) * MXU_TILE)
        halves.append(_dot(hi[:, sl], bd) + _dot(lo[:, sl], bd))
    ss = jnp.concatenate(halves, axis=1)
    return k * lax.rsqrt(ss * (1.0 / HEAD_DIM) + EPS) * kw_ref[...]


def _gate_prefix(fg_t, bfg_ref, tri_ref, carry, n_valid):
    t = fg_t.shape[1]
    logf = jax.nn.log_sigmoid(fg_t + _tile_lanes(bfg_ref[...], t))
    if n_valid < t:
        lane = lax.broadcasted_iota(jnp.int32, logf.shape, 1)
        logf = jnp.where(lane < n_valid, logf, 0.0)
    parts = _split_bf16(logf, N_SPLIT)
    pad = jnp.zeros((2 * GATE_ROWS - N_SPLIT * N_HEADS, t), jnp.float32)
    stacked = _bf16(jnp.concatenate(parts + [pad], axis=0))
    cs = _dot(stacked, tri_ref[...])
    c = _tile_lanes(carry, t) + cs[0:N_HEADS]
    for p in range(1, N_SPLIT):
        c = c + cs[p * N_HEADS:(p + 1) * N_HEADS]
    total = jnp.sum(logf, axis=1, keepdims=True)
    return c, total


def _key_bias_columns(c, n_valid):
    t = c.shape[1]
    parts = _split_bf16(c * -LOG2E, N_SPLIT)
    if n_valid < t:
        lane = lax.broadcasted_iota(jnp.int32, c.shape, 1)
        parts[0] = jnp.where(lane < n_valid, parts[0], MASKED)
    pad = jnp.zeros((LANES - N_SPLIT * N_HEADS, t), jnp.float32)
    return _bf16(jnp.concatenate(parts + [pad], axis=0).T)


def _meta_inproj_kernel(h_ref, n1w_ref, wt_ref, wku_ref, bfg_ref, kw_ref, bd_ref, tri_ref,
                        k_ref, cp_ref, vt_ref, u_ref, c_ref):
    proj_t, ku = _norm_project(h_ref[...], n1w_ref, wt_ref, wku_ref)
    vt_ref[...] = _bf16(proj_t[D_ATTN:2 * D_ATTN])
    k_ref[...] = _bf16(_key_norm(ku[:, 0:D_ATTN], kw_ref, bd_ref))
    u_ref[...] = ku[0:N_META, D_ATTN:]
    zero = jnp.zeros((N_HEADS, LANES), jnp.float32)
    c, total = _gate_prefix(proj_t[2 * D_ATTN:2 * D_ATTN + N_HEADS], bfg_ref, tri_ref, zero, N_META)
    cp_ref[...] = _key_bias_columns(c, N_META)
    c_ref[...] = jnp.broadcast_to(total, (N_HEADS, LANES))


def _inproj_kernel(x_ref, um_ref, cm_ref, n1w_ref, wt_ref, wku_ref, bfg_ref, qw_ref, kw_ref, bd_ref,
                   tri_ref, wp_ref, ps_ref,
                   qt_ref, k_ref, cp_ref, vt_ref, po_ref,
                   ubuf_ref, carry_ref):
    t = INPROJ_SUB

    @pl.when(pl.program_id(1) == 0)
    def _():
        ubuf_ref[0:N_META, :] = um_ref[...]
        carry_ref[...] = cm_ref[...]

    carry = carry_ref[...]
    n_sub = x_ref.shape[0] // t

    def project(r):
        return _norm_project(x_ref[r * t:(r + 1) * t, :], n1w_ref, wt_ref, wku_ref)

    projected = project(0)
    for r in range(n_sub):
        rows = slice(r * t, (r + 1) * t)
        proj_t, ku = projected
        if r + 1 < n_sub:
            projected = project(r + 1)
        vt_ref[:, rows] = _bf16(proj_t[D_ATTN:2 * D_ATTN])

        q3 = proj_t[0:D_ATTN].reshape(N_HEADS, HEAD_DIM, t)
        q3 = q3 * lax.rsqrt(jnp.mean(q3 * q3, axis=1, keepdims=True) + EPS)
        qt_ref[:, rows] = _bf16(q3.reshape(D_ATTN, t) * _tile_lanes(qw_ref[...], t))

        k_ref[rows, :] = _bf16(_key_norm(ku[:, 0:D_ATTN], kw_ref, bd_ref))

        c, total = _gate_prefix(proj_t[2 * D_ATTN:2 * D_ATTN + N_HEADS], bfg_ref, tri_ref, carry, t)
        cp_ref[rows, :] = _key_bias_columns(c, t)
        carry = carry + total

        base = N_META + r * t
        ubuf_ref[base:base + t, :] = ku[:, D_ATTN:]
        groups = []
        for g, w in enumerate(POOL_WINDOWS):
            lanes = slice(g * POOL_GROUP_DIM, (g + 1) * POOL_GROUP_DIM)
            cur = ubuf_ref[base:base + t, lanes]
            acc = cur
            for j in range(1, w):
                acc = acc + ubuf_ref[base - j:base - j + t, lanes]
            groups.append(acc * (1.0 / w) - cur)
        pooled = _bf16(jnp.concatenate(groups, axis=1))
        halves = []
        for c2 in range(D_POOL // MXU_TILE):
            halves.append(_dot(pooled[:, c2 * MXU_TILE:(c2 + 1) * MXU_TILE], wp_ref[c2]))
        po_ref[rows, :] = _bf16(jnp.concatenate(halves, axis=1) * ps_ref[...])
    carry_ref[...] = carry
    ubuf_ref[0:N_META, :] = ubuf_ref[x_ref.shape[0]:x_ref.shape[0] + N_META, :]


def _attn_kernel(qt_ref, k_ref, cp_ref, vt_ref, km_ref, cpm_ref, vtm_ref,
                 wo_f32_ref, wg_f32_ref, wu_f32_ref, wd_f32_ref,
                 o_ref, wo_ref, wg_ref, wu_ref, wd_ref,
                 qa_ref, s_ref, cmax_ref, m_ref, acc_ref):
    pair = pl.program_id(1)
    bq = Q_BLOCK
    bk = KV_BLOCK
    n_q = o_ref.shape[0] // bq
    heads = range(HEADS_PER_STEP)
    ones = jnp.ones((ONES_ROWS, bk), jnp.bfloat16)
    meta_slot = 2

    def init_q_aug():
        row = lax.broadcasted_iota(jnp.int32, (LANES, bq), 0)
        for hh in heads:
            head = pair * HEADS_PER_STEP + hh
            sel = ((row & (N_HEADS - 1)) == head) & (row < N_SPLIT * N_HEADS)
            const = jnp.concatenate([jnp.zeros((LANES, bq), jnp.bfloat16),
                                     jnp.where(sel, 1.0, 0.0).astype(jnp.bfloat16)], axis=0)
            for qbuf in range(2):
                qa_ref[qbuf, hh] = const

    def build_q_aug(blk, qbuf):
        start = pl.multiple_of(blk * bq, bq)
        for hh in heads:
            own = slice(hh * HEAD_DIM, (hh + 1) * HEAD_DIM)
            qa_ref[qbuf, hh, own, :] = qt_ref[own, pl.ds(start, bq)]

    def scores(k_aug, slot, hh, cols, qbuf, diagonal=False):
        rows = k_aug.shape[0]
        live = min(rows, cols.stop) if diagonal else rows
        s = _dot(k_aug[0:live], qa_ref[qbuf, hh, :, cols])
        if diagonal:
            causal = (lax.broadcasted_iota(jnp.int32, s.shape, 0)
                      <= lax.broadcasted_iota(jnp.int32, s.shape, 1) + cols.start)
            s = jnp.where(causal, s, MASKED)
            if live < rows:
                s_ref[slot, hh, live:rows, cols] = jnp.full((rows - live, cols.stop - cols.start),
                                                            MASKED, jnp.float32)
        s_ref[slot, hh, 0:live, cols] = s
        cmax_ref[slot, hh, :, cols] = jnp.max(s, axis=0, keepdims=True)

    def update(v_blk, slot, hh, cols, between=None):
        rows = v_blk.shape[1]
        m_old = m_ref[hh, :, cols]
        m_new = jnp.maximum(m_old, cmax_ref[slot, hh, :, cols])
        v_aug = jnp.concatenate([v_blk, ones[:, 0:rows]], axis=0)
        if between is None or rows <= MXU_TILE:
            if between is not None:
                between()
            p = _bf16(jnp.exp2(s_ref[slot, hh, 0:rows, cols] - m_new))
            acc_ref[hh, :, cols] = jnp.exp2(m_old - m_new) * acc_ref[hh, :, cols] + _dot(v_aug, p)
        else:
            half = rows // 2
            p = _bf16(jnp.exp2(s_ref[slot, hh, 0:half, cols] - m_new))
            acc_ref[hh, :, cols] = (jnp.exp2(m_old - m_new) * acc_ref[hh, :, cols]
                                    + _dot(v_aug[:, 0:half], p))
            between()
            p = _bf16(jnp.exp2(s_ref[slot, hh, half:rows, cols] - m_new))
            acc_ref[hh, :, cols] = acc_ref[hh, :, cols] + _dot(v_aug[:, half:rows], p)
        m_ref[hh, :, cols] = m_new

    def k_tile(n):
        start = pl.multiple_of(n * bk, bk)
        return jnp.concatenate([k_ref[pl.ds(start, bk), :], cp_ref[pl.ds(start, bk), :]], axis=1)

    def v_tile(n):
        start = pl.multiple_of(n * bk, bk)
        return lambda hh: vt_ref[hh * HEAD_DIM:(hh + 1) * HEAD_DIM, pl.ds(start, bk)]

    def k_meta():
        return jnp.concatenate([km_ref[0:N_META, :], cpm_ref[0:N_META, :]], axis=1)

    def v_meta(hh):
        return vtm_ref[hh * HEAD_DIM:(hh + 1) * HEAD_DIM, 0:N_META]

    def stage(k_aug, s_slot, v_blk, u_slot, qbuf=None, diagonal=False):
        for hh in heads:
            for c in range(bq // Q_CHUNK):
                cols = slice(c * Q_CHUNK, (c + 1) * Q_CHUNK)
                if k_aug is not None and v_blk is not None:
                    update(v_blk(hh), u_slot, hh, cols,
                           between=lambda: scores(k_aug, s_slot, hh, cols, qbuf, diagonal))
                elif k_aug is not None:
                    scores(k_aug, s_slot, hh, cols, qbuf, diagonal)
                elif v_blk is not None:
                    update(v_blk(hh), u_slot, hh, cols)

    def reset_state():
        for hh in heads:
            m_ref[hh] = jnp.full(m_ref.shape[1:], MASKED, jnp.float32)
            acc_ref[hh] = jnp.zeros(acc_ref.shape[1:], jnp.float32)

    def write_block(qi):
        outs = []
        for hh in heads:
            acc = acc_ref[hh]
            outs.append(acc[0:HEAD_DIM] / acc[HEAD_DIM:HEAD_DIM + 1])
        o_ref[pl.ds(pl.multiple_of(qi * bq, bq), bq), :] = _bf16(jnp.concatenate(outs, axis=0).T)

    for src, dst in ((wo_f32_ref, wo_ref), (wg_f32_ref, wg_ref), (wu_f32_ref, wu_ref),
                     (wd_f32_ref, wd_ref)):
        dst[...] = _bf16(src[...])

    init_q_aug()
    build_q_aug(0, 0)
    reset_state()
    stage(k_tile(0), 0, None, None, 0, diagonal=True)
    stage(k_meta(), meta_slot, None, None, 0)

    def query_block(qi, carry):
        def tile_at(j):
            return jnp.where(j == 0, qi, j - 1)

        first_dyn = lax.shift_right_logical(qi + 1, 1) & 1
        parity_dyn = qi & 1
        for first in range(2):
            @pl.when(first_dyn == first)
            def _(first=first):
                def run_stages(j0, count):
                    for t in range(count):
                        slots = (1 - first, first) if t % 2 == 0 else (first, 1 - first)
                        tile = tile_at(j0) if t == 0 else j0 + t - 1
                        stage(k_tile(j0 + t), slots[0], v_tile(tile), slots[1], parity_dyn)

                n_long = lax.shift_right_logical(qi, 2)
                lax.fori_loop(0, n_long, lambda i, c: (run_stages(LOOP_STAGES * i, LOOP_STAGES), c)[1], 0)
                lax.fori_loop(0, lax.shift_right_logical(qi, 1) & 1,
                              lambda i, c: (run_stages(LOOP_STAGES * n_long, 2), c)[1], 0)

        next_blk = jnp.minimum(qi + 1, n_q - 1)
        for variant in range(4):
            @pl.when((qi & 3) == variant)
            def _(variant=variant):
                first = ((variant + 1) >> 1) & 1
                parity = variant % 2
                build_q_aug(next_blk, 1 - parity)
                last = first
                if parity:
                    stage(k_tile(qi - 1), 1 - first, v_tile(tile_at(qi - 1)), first, parity)
                    last = 1 - first
                stage(None, None, v_meta, meta_slot + parity)
                stage(k_tile(next_blk), 1 - last, v_tile(tile_at(qi)), last, 1 - parity,
                      diagonal=True)
                stage(k_meta(), meta_slot + 1 - parity, None, None, 1 - parity)
                write_block(qi)
                reset_state()

        return carry

    lax.fori_loop(0, n_q, query_block, 0)


def _out_ffn_kernel(x_ref, a_ref, p_ref, wo_ref, n2w_ref, wg_ref, wu_ref, wd_ref, o_ref):
    t = FFN_SUB
    n_sub = x_ref.shape[0] // t

    def mixed(r):
        rows = slice(r * t, (r + 1) * t)
        mix = _dot(jnp.concatenate([a_ref[rows, :], p_ref[rows, :]], axis=1), wo_ref[...])
        h1 = x_ref[rows, :] + mix
        return h1, _bf16(h1 * _rms_scale(h1) * n2w_ref[...])

    nxt = mixed(0)
    for r in range(n_sub):
        h1, n2 = nxt
        if r + 1 < n_sub:
            nxt = mixed(r + 1)
        acc = h1
        lo = 0
        for width in FF_CHUNKS:
            g = _dot(n2, wg_ref[:, lo:lo + width])
            u = _dot(n2, wu_ref[:, lo:lo + width])
            acc = acc + _dot(_bf16(g * jax.nn.sigmoid(g) * u), wd_ref[lo:lo + width, :])
            lo += width
        o_ref[r * t:(r + 1) * t, :] = acc


def _const_spec(shape, single_buffer=False):
    zeros = (0,) * len(shape)
    if single_buffer:
        return pl.BlockSpec(shape, lambda *_: zeros, pipeline_mode=pl.Buffered(1))
    return pl.BlockSpec(shape, lambda *_: zeros)


def _block_diag(blocks):
    n, r, c = blocks.shape
    out = jnp.zeros((n * r, n * c), blocks.dtype)
    for i in range(n):
        out = out.at[i * r:(i + 1) * r, i * c:(i + 1) * c].set(blocks[i])
    return out


def kernel(x, meta_tokens, norm1_w, w_in, b_fgate, q_norm_w, k_norm_w, w_pool, pool_scale, w_out,
           norm2_w, w_gate, w_up, w_down):
    batch, seq, d_model = x.shape
    assert w_in.shape[0] == 1, "one layer"
    assert seq % INPROJ_BLOCK == 0 and INPROJ_BLOCK % INPROJ_SUB == 0
    assert seq % ROW_BLOCK == 0 and seq % Q_BLOCK == 0 and Q_BLOCK == KV_BLOCK
    d_ff = w_gate.shape[-1]
    assert sum(FF_CHUNKS) == d_ff
    f32, bf16 = jnp.float32, jnp.bfloat16

    w = w_in[0]
    w_q, w_k, w_v = (w[:, i * D_ATTN:(i + 1) * D_ATTN] for i in range(3))
    w_fg = w[:, 3 * D_ATTN:3 * D_ATTN + N_HEADS]
    w_u = w[:, 3 * D_ATTN + N_HEADS:]
    wt = jnp.concatenate([w_q.T, w_v.T, w_fg.T, jnp.zeros((GATE_ROWS - N_HEADS, d_model), f32)],
                         axis=0).astype(bf16)
    wku = jnp.concatenate([w_k, w_u], axis=1).astype(bf16)
    n1w = norm1_w[0].reshape(1, d_model)
    n2w = norm2_w[0].reshape(1, d_model)
    bfg = jnp.broadcast_to(b_fgate[0].reshape(N_HEADS, 1), (N_HEADS, LANES))
    qw = jnp.broadcast_to(jnp.tile(q_norm_w[0] * (HEAD_DIM ** -0.5 * LOG2E), N_HEADS).reshape(D_ATTN, 1),
                          (D_ATTN, LANES))
    kw = jnp.tile(k_norm_w[0], N_HEADS).reshape(1, D_ATTN)
    bd = jnp.asarray(np.kron(np.eye(MXU_TILE // HEAD_DIM, dtype=np.float32),
                             np.ones((HEAD_DIM, HEAD_DIM), np.float32)), bf16)
    wp = jnp.stack([_block_diag(w_pool[0, 2 * i:2 * i + 2]) for i in range(D_POOL // MXU_TILE)]
                   ).astype(bf16)
    ps = pool_scale[0].reshape(1, D_POOL)

    def upper_tri(n):
        return jnp.asarray(np.triu(np.ones((n, n), np.float32)), bf16)

    meta_blk = jnp.zeros((META_BLOCK, d_model), f32).at[0:N_META].set(meta_tokens)
    n_proj_t = 2 * D_ATTN + GATE_ROWS

    km, cpm, vtm, um, cm = pl.pallas_call(
        _meta_inproj_kernel,
        out_shape=(jax.ShapeDtypeStruct((META_BLOCK, D_ATTN), bf16),
                   jax.ShapeDtypeStruct((META_BLOCK, LANES), bf16),
                   jax.ShapeDtypeStruct((D_ATTN, META_BLOCK), bf16),
                   jax.ShapeDtypeStruct((N_META, D_POOL), f32),
                   jax.ShapeDtypeStruct((N_HEADS, LANES), f32)),
        compiler_params=pltpu.CompilerParams(vmem_limit_bytes=VMEM_LIMIT_BYTES),
        name="meta_inproj",
    )(meta_blk, n1w, wt, wku, bfg, kw, bd, upper_tri(META_BLOCK))

    tok = lambda b, j: (b, j, 0)
    tok_t = lambda b, j: (b, 0, j)
    qt, k, cp, vt, po = pl.pallas_call(
        _inproj_kernel,
        grid=(batch, seq // INPROJ_BLOCK),
        in_specs=[pl.BlockSpec((None, INPROJ_BLOCK, d_model), tok),
                  _const_spec((N_META, D_POOL)), _const_spec((N_HEADS, LANES)),
                  _const_spec((1, d_model)), _const_spec((n_proj_t, d_model)),
                  _const_spec((d_model, D_ATTN + D_POOL)), _const_spec((N_HEADS, LANES)),
                  _const_spec((D_ATTN, LANES)), _const_spec((1, D_ATTN)),
                  _const_spec((MXU_TILE, MXU_TILE)), _const_spec((INPROJ_SUB, INPROJ_SUB)),
                  _const_spec((D_POOL // MXU_TILE, MXU_TILE, MXU_TILE)), _const_spec((1, D_POOL))],
        out_specs=(pl.BlockSpec((None, D_ATTN, INPROJ_BLOCK), tok_t),
                   pl.BlockSpec((None, INPROJ_BLOCK, D_ATTN), tok),
                   pl.BlockSpec((None, INPROJ_BLOCK, LANES), tok),
                   pl.BlockSpec((None, D_ATTN, INPROJ_BLOCK), tok_t),
                   pl.BlockSpec((None, INPROJ_BLOCK, D_POOL), tok)),
        out_shape=(jax.ShapeDtypeStruct((batch, D_ATTN, seq), bf16),
                   jax.ShapeDtypeStruct((batch, seq, D_ATTN), bf16),
                   jax.ShapeDtypeStruct((batch, seq, LANES), bf16),
                   jax.ShapeDtypeStruct((batch, D_ATTN, seq), bf16),
                   jax.ShapeDtypeStruct((batch, seq, D_POOL), bf16)),
        scratch_shapes=[pltpu.VMEM((N_META + INPROJ_BLOCK, D_POOL), f32),
                        pltpu.VMEM((N_HEADS, LANES), f32)],
        compiler_params=pltpu.CompilerParams(
            dimension_semantics=("arbitrary", "arbitrary"), vmem_limit_bytes=VMEM_LIMIT_BYTES),
        name="inproj",
    )(x, um, cm, n1w, wt, wku, bfg, qw, kw, bd, upper_tri(INPROJ_SUB), wp, ps)

    n_pairs = N_HEADS // HEADS_PER_STEP
    ffn_weights = (w_out[0], w_gate[0], w_up[0], w_down[0])
    n_steps = batch * n_pairs
    slab = lambda b, p: (b * n_pairs + p, 0)
    slab_specs = []
    for wgt in ffn_weights:
        assert wgt.shape[0] % (n_steps * 16) == 0, "row slabs must be whole bf16 sublane tiles"
        slab_specs.append(pl.BlockSpec((wgt.shape[0] // n_steps, wgt.shape[1]), slab))
    a, wo, wg, wu, wd = pl.pallas_call(
        _attn_kernel,
        grid=(batch, n_pairs),
        in_specs=[pl.BlockSpec((None, LANES, seq), lambda b, p: (b, p, 0)),
                  pl.BlockSpec((None, seq, LANES), lambda b, p: (b, 0, p)),
                  pl.BlockSpec((None, seq, LANES), lambda b, p: (b, 0, 0)),
                  pl.BlockSpec((None, LANES, seq), lambda b, p: (b, p, 0)),
                  pl.BlockSpec((META_BLOCK, LANES), lambda b, p: (0, p)),
                  pl.BlockSpec((META_BLOCK, LANES), lambda b, p: (0, 0)),
                  pl.BlockSpec((LANES, META_BLOCK), lambda b, p: (p, 0))] + slab_specs,
        out_specs=[pl.BlockSpec((None, seq, LANES), lambda b, p: (b, 0, p))] + slab_specs,
        out_shape=[jax.ShapeDtypeStruct((batch, seq, D_ATTN), bf16)]
        + [jax.ShapeDtypeStruct(wgt.shape, bf16) for wgt in ffn_weights],
        scratch_shapes=[pltpu.VMEM((2, HEADS_PER_STEP, 2 * LANES, Q_BLOCK), bf16),
                        pltpu.VMEM((4, HEADS_PER_STEP, KV_BLOCK, Q_BLOCK), f32),
                        pltpu.VMEM((4, HEADS_PER_STEP, 1, Q_BLOCK), f32),
                        pltpu.VMEM((HEADS_PER_STEP, 1, Q_BLOCK), f32),
                        pltpu.VMEM((HEADS_PER_STEP, HEAD_DIM + ONES_ROWS, Q_BLOCK), f32)],
        compiler_params=pltpu.CompilerParams(
            dimension_semantics=("parallel", "parallel"), vmem_limit_bytes=VMEM_LIMIT_BYTES),
        name="fox_attn",
    )(qt, k, cp, vt, km, cpm, vtm, *ffn_weights)

    rows = batch * seq
    row = lambda r: (r, 0)
    out = pl.pallas_call(
        _out_ffn_kernel,
        grid=(rows // ROW_BLOCK,),
        in_specs=[pl.BlockSpec((ROW_BLOCK, d_model), row),
                  pl.BlockSpec((ROW_BLOCK, D_ATTN), row),
                  pl.BlockSpec((ROW_BLOCK, D_POOL), row),
                  _const_spec((D_ATTN + D_POOL, d_model), True), _const_spec((1, d_model)),
                  _const_spec((d_model, d_ff), True), _const_spec((d_model, d_ff), True),
                  _const_spec((d_ff, d_model), True)],
        out_specs=pl.BlockSpec((ROW_BLOCK, d_model), row),
        out_shape=jax.ShapeDtypeStruct((rows, d_model), f32),
        compiler_params=pltpu.CompilerParams(
            dimension_semantics=("parallel",), vmem_limit_bytes=VMEM_LIMIT_BYTES),
        name="out_ffn",
    )(x.reshape(rows, d_model), a.reshape(rows, D_ATTN), po.reshape(rows, D_POOL),
      wo, n2w, wg, wu, wd)
    return out.reshape(batch, seq, d_model)
```

```python
import numpy as np
import jax
import jax.numpy as jnp
from jax import lax
from jax.experimental import pallas as pl
from jax.experimental.pallas import tpu as pltpu

N_META = 16
N_HEADS = 8
HEAD_DIM = 64
D_ATTN = N_HEADS * HEAD_DIM
POOL_WINDOWS = (2, 4, 8, 16)
POOL_GROUP_DIM = 128
D_POOL = len(POOL_WINDOWS) * POOL_GROUP_DIM
EPS = 1e-6

LANES = 128
MXU_TILE = 256
HEADS_PER_STEP = LANES // HEAD_DIM
N_SPLIT = 3
GATE_ROWS = 16
ONES_ROWS = 16
MASKED = -1e30
LOG2E = 1.4426950408889634

META_BLOCK = 128
INPROJ_BLOCK = 1024
INPROJ_SUB = 256
ROW_BLOCK = 1024
FFN_SUB = 256
Q_BLOCK = 512
KV_BLOCK = 512
Q_CHUNK = MXU_TILE
LOOP_STAGES = 4
FF_CHUNKS = (1024, 1024, 768)
VMEM_LIMIT_BYTES = 56 * 1024 * 1024


def _f32(x):
    return x.astype(jnp.float32)


def _bf16(x):
    return x.astype(jnp.bfloat16)


def _dot(a, b):
    return jnp.dot(a, b, preferred_element_type=jnp.float32)


def _dot_nt(a, b):
    return lax.dot_general(a, b, (((1,), (1,)), ((), ())), preferred_element_type=jnp.float32)


def _split_bf16(x, n):
    parts = []
    r = x
    for _ in range(n):
        p = _f32(_bf16(r))
        parts.append(p)
        r = r - p
    return parts


def _rms_scale(x):
    return lax.rsqrt(jnp.mean(x * x, axis=-1, keepdims=True) + EPS)


def _tile_lanes(x, n):
    return jnp.concatenate([x] * (n // x.shape[1]), axis=1) if n != x.shape[1] else x


def _norm_project(h, n1w_ref, wt_ref, wku_ref):
    n1 = _bf16(h * _rms_scale(h) * n1w_ref[...])
    proj_t = _dot_nt(wt_ref[...], n1)
    ku = _dot(n1, wku_ref[...])
    return proj_t, ku


def _key_norm(k, kw_ref, bd_ref):
    sq = _bf16(k * k)
    bd = bd_ref[...]
    halves = []
    for c in range(D_ATTN // MXU_TILE):
        sl = slice(c * MXU_TILE, (c + 1) * MXU_TILE)
        halves.append(_dot(sq[:, sl], bd))
    ss = jnp.concatenate(halves, axis=1)
    return k * lax.rsqrt(ss * (1.0 / HEAD_DIM) + EPS) * kw_ref[...]


def _gate_prefix(fg_t, bfg_ref, tri_ref, carry, n_valid):
    t = fg_t.shape[1]
    logf = jax.nn.log_sigmoid(fg_t + _tile_lanes(bfg_ref[...], t))
    if n_valid < t:
        lane = lax.broadcasted_iota(jnp.int32, logf.shape, 1)
        logf = jnp.where(lane < n_valid, logf, 0.0)
    parts = _split_bf16(logf, N_SPLIT)
    pad = jnp.zeros((2 * GATE_ROWS - N_SPLIT * N_HEADS, t), jnp.float32)
    stacked = _bf16(jnp.concatenate(parts + [pad], axis=0))
    cs = _dot(stacked, tri_ref[...])
    c = _tile_lanes(carry, t) + cs[0:N_HEADS]
    for p in range(1, N_SPLIT):
        c = c + cs[p * N_HEADS:(p + 1) * N_HEADS]
    total = jnp.sum(logf, axis=1, keepdims=True)
    return c, total


def _key_bias_columns(c, n_valid):
    t = c.shape[1]
    parts = _split_bf16(c * -LOG2E, N_SPLIT)
    if n_valid < t:
        lane = lax.broadcasted_iota(jnp.int32, c.shape, 1)
        parts[0] = jnp.where(lane < n_valid, parts[0], MASKED)
    pad = jnp.zeros((LANES - N_SPLIT * N_HEADS, t), jnp.float32)
    return _bf16(jnp.concatenate(parts + [pad], axis=0).T)


def _meta_inproj_kernel(h_ref, n1w_ref, wt_ref, wku_ref, bfg_ref, kw_ref, bd_ref, tri_ref,
                        k_ref, cp_ref, vt_ref, u_ref, c_ref):
    proj_t, ku = _norm_project(h_ref[...], n1w_ref, wt_ref, wku_ref)
    vt_ref[...] = _bf16(proj_t[D_ATTN:2 * D_ATTN])
    k_ref[...] = _bf16(_key_norm(ku[:, 0:D_ATTN], kw_ref, bd_ref))
    u_ref[...] = ku[0:N_META, D_ATTN:]
    zero = jnp.zeros((N_HEADS, LANES), jnp.float32)
    c, total = _gate_prefix(proj_t[2 * D_ATTN:2 * D_ATTN + N_HEADS], bfg_ref, tri_ref, zero, N_META)
    cp_ref[...] = _key_bias_columns(c, N_META)
    c_ref[...] = jnp.broadcast_to(total, (N_HEADS, LANES))


def _inproj_kernel(x_ref, um_ref, cm_ref, n1w_ref, wt_ref, wku_ref, bfg_ref, qw_ref, kw_ref, bd_ref,
                   tri_ref, wp_ref, ps_ref,
                   qt_ref, k_ref, cp_ref, vt_ref, po_ref,
                   ubuf_ref, carry_ref):
    t = INPROJ_SUB

    @pl.when(pl.program_id(1) == 0)
    def _():
        ubuf_ref[0:N_META, :] = um_ref[...]
        carry_ref[...] = cm_ref[...]

    carry = carry_ref[...]
    n_sub = x_ref.shape[0] // t

    def project(r):
        return _norm_project(x_ref[r * t:(r + 1) * t, :], n1w_ref, wt_ref, wku_ref)

    projected = project(0)
    for r in range(n_sub):
        rows = slice(r * t, (r + 1) * t)
        proj_t, ku = projected
        if r + 1 < n_sub:
            projected = project(r + 1)
        vt_ref[:, rows] = _bf16(proj_t[D_ATTN:2 * D_ATTN])

        q3 = proj_t[0:D_ATTN].reshape(N_HEADS, HEAD_DIM, t)
        q3 = q3 * lax.rsqrt(jnp.mean(q3 * q3, axis=1, keepdims=True) + EPS)
        qt_ref[:, rows] = _bf16(q3.reshape(D_ATTN, t) * _tile_lanes(qw_ref[...], t))

        k_ref[rows, :] = _bf16(_key_norm(ku[:, 0:D_ATTN], kw_ref, bd_ref))

        c, total = _gate_prefix(proj_t[2 * D_ATTN:2 * D_ATTN + N_HEADS], bfg_ref, tri_ref, carry, t)
        cp_ref[rows, :] = _key_bias_columns(c, t)
        carry = carry + total

        base = N_META + r * t
        ubuf_ref[base:base + t, :] = ku[:, D_ATTN:]
        groups = []
        for g, w in enumerate(POOL_WINDOWS):
            lanes = slice(g * POOL_GROUP_DIM, (g + 1) * POOL_GROUP_DIM)
            cur = ubuf_ref[base:base + t, lanes]
            acc = cur
            for j in range(1, w):
                acc = acc + ubuf_ref[base - j:base - j + t, lanes]
            groups.append(acc * (1.0 / w) - cur)
        pooled = _bf16(jnp.concatenate(groups, axis=1))
        halves = []
        for c2 in range(D_POOL // MXU_TILE):
            halves.append(_dot(pooled[:, c2 * MXU_TILE:(c2 + 1) * MXU_TILE], wp_ref[c2]))
        po_ref[rows, :] = _bf16(jnp.concatenate(halves, axis=1) * ps_ref[...])
    carry_ref[...] = carry
    ubuf_ref[0:N_META, :] = ubuf_ref[x_ref.shape[0]:x_ref.shape[0] + N_META, :]


def _attn_kernel(qt_ref, k_ref, cp_ref, vt_ref, km_ref, cpm_ref, vtm_ref,
                 wo_f32_ref, wg_f32_ref, wu_f32_ref, wd_f32_ref,
                 o_ref, wo_ref, wg_ref, wu_ref, wd_ref,
                 qa_ref, s_ref, cmax_ref, m_ref, acc_ref):
    pair = pl.program_id(1)
    bq = Q_BLOCK
    bk = KV_BLOCK
    n_q = o_ref.shape[0] // bq
    heads = range(HEADS_PER_STEP)
    ones = jnp.ones((ONES_ROWS, bk), jnp.bfloat16)
    meta_slot = 2

    def init_q_aug():
        row = lax.broadcasted_iota(jnp.int32, (LANES, bq), 0)
        for hh in heads:
            head = pair * HEADS_PER_STEP + hh
            sel = ((row & (N_HEADS - 1)) == head) & (row < N_SPLIT * N_HEADS)
            const = jnp.concatenate([jnp.zeros((LANES, bq), jnp.bfloat16),
                                     jnp.where(sel, 1.0, 0.0).astype(jnp.bfloat16)], axis=0)
            for qbuf in range(2):
                qa_ref[qbuf, hh] = const

    def build_q_aug(blk, qbuf):
        start = pl.multiple_of(blk * bq, bq)
        for hh in heads:
            own = slice(hh * HEAD_DIM, (hh + 1) * HEAD_DIM)
            qa_ref[qbuf, hh, own, :] = qt_ref[own, pl.ds(start, bq)]

    def scores(k_aug, slot, hh, cols, qbuf, diagonal=False):
        rows = k_aug.shape[0]
        live = min(rows, cols.stop) if diagonal else rows
        s = _dot(k_aug[0:live], qa_ref[qbuf, hh, :, cols])
        if diagonal:
            causal = (lax.broadcasted_iota(jnp.int32, s.shape, 0)
                      <= lax.broadcasted_iota(jnp.int32, s.shape, 1) + cols.start)
            s = jnp.where(causal, s, MASKED)
            if live < rows:
                s_ref[slot, hh, live:rows, cols] = jnp.full((rows - live, cols.stop - cols.start),
                                                            MASKED, jnp.float32)
        s_ref[slot, hh, 0:live, cols] = s
        cmax_ref[slot, hh, :, cols] = jnp.max(s, axis=0, keepdims=True)

    def update(v_blk, slot, hh, cols, between=None):
        rows = v_blk.shape[1]
        m_old = m_ref[hh, :, cols]
        m_new = jnp.maximum(m_old, cmax_ref[slot, hh, :, cols])
        v_aug = jnp.concatenate([v_blk, ones[:, 0:rows]], axis=0)
        if between is None or rows <= MXU_TILE:
            if between is not None:
                between()
            p = _bf16(jnp.exp2(s_ref[slot, hh, 0:rows, cols] - m_new))
            acc_ref[hh, :, cols] = jnp.exp2(m_old - m_new) * acc_ref[hh, :, cols] + _dot(v_aug, p)
        else:
            half = rows // 2
            p = _bf16(jnp.exp2(s_ref[slot, hh, 0:half, cols] - m_new))
            acc_ref[hh, :, cols] = (jnp.exp2(m_old - m_new) * acc_ref[hh, :, cols]
                                    + _dot(v_aug[:, 0:half], p))
            between()
            p = _bf16(jnp.exp2(s_ref[slot, hh, half:rows, cols] - m_new))
            acc_ref[hh, :, cols] = acc_ref[hh, :, cols] + _dot(v_aug[:, half:rows], p)
        m_ref[hh, :, cols] = m_new

    def k_tile(n):
        start = pl.multiple_of(n * bk, bk)
        return jnp.concatenate([k_ref[pl.ds(start, bk), :], cp_ref[pl.ds(start, bk), :]], axis=1)

    def v_tile(n):
        start = pl.multiple_of(n * bk, bk)
        return lambda hh: vt_ref[hh * HEAD_DIM:(hh + 1) * HEAD_DIM, pl.ds(start, bk)]

    def k_meta():
        return jnp.concatenate([km_ref[0:N_META, :], cpm_ref[0:N_META, :]], axis=1)

    def v_meta(hh):
        return vtm_ref[hh * HEAD_DIM:(hh + 1) * HEAD_DIM, 0:N_META]

    def stage(k_aug, s_slot, v_blk, u_slot, qbuf=None, diagonal=False):
        for hh in heads:
            for c in range(bq // Q_CHUNK):
                cols = slice(c * Q_CHUNK, (c + 1) * Q_CHUNK)
                if k_aug is not None and v_blk is not None:
                    update(v_blk(hh), u_slot, hh, cols,
                           between=lambda: scores(k_aug, s_slot, hh, cols, qbuf, diagonal))
                elif k_aug is not None:
                    scores(k_aug, s_slot, hh, cols, qbuf, diagonal)
                elif v_blk is not None:
                    update(v_blk(hh), u_slot, hh, cols)

    def reset_state():
        for hh in heads:
            m_ref[hh] = jnp.full(m_ref.shape[1:], MASKED, jnp.float32)
            acc_ref[hh] = jnp.zeros(acc_ref.shape[1:], jnp.float32)

    def write_block(qi):
        outs = []
        for hh in heads:
            acc = acc_ref[hh]
            outs.append(acc[0:HEAD_DIM] / acc[HEAD_DIM:HEAD_DIM + 1])
        o_ref[pl.ds(pl.multiple_of(qi * bq, bq), bq), :] = _bf16(jnp.concatenate(outs, axis=0).T)

    for src, dst in ((wo_f32_ref, wo_ref), (wg_f32_ref, wg_ref), (wu_f32_ref, wu_ref),
                     (wd_f32_ref, wd_ref)):
        dst[...] = _bf16(src[...])

    init_q_aug()
    build_q_aug(0, 0)
    reset_state()
    stage(k_tile(0), 0, None, None, 0, diagonal=True)
    stage(k_meta(), meta_slot, None, None, 0)

    def query_block(qi, carry):
        def tile_at(j):
            return jnp.where(j == 0, qi, j - 1)

        first_dyn = lax.shift_right_logical(qi + 1, 1) & 1
        parity_dyn = qi & 1
        for first in range(2):
            @pl.when(first_dyn == first)
            def _(first=first):
                def run_stages(j0, count):
                    for t in range(count):
                        slots = (1 - first, first) if t % 2 == 0 else (first, 1 - first)
                        tile = tile_at(j0) if t == 0 else j0 + t - 1
                        stage(k_tile(j0 + t), slots[0], v_tile(tile), slots[1], parity_dyn)

                n_long = lax.shift_right_logical(qi, 2)
                lax.fori_loop(0, n_long, lambda i, c: (run_stages(LOOP_STAGES * i, LOOP_STAGES), c)[1], 0)
                lax.fori_loop(0, lax.shift_right_logical(qi, 1) & 1,
                              lambda i, c: (run_stages(LOOP_STAGES * n_long, 2), c)[1], 0)

        next_blk = jnp.minimum(qi + 1, n_q - 1)
        for variant in range(4):
            @pl.when((qi & 3) == variant)
            def _(variant=variant):
                first = ((variant + 1) >> 1) & 1
                parity = variant % 2
                build_q_aug(next_blk, 1 - parity)
                last = first
                if parity:
                    stage(k_tile(qi - 1), 1 - first, v_tile(tile_at(qi - 1)), first, parity)
                    last = 1 - first
                stage(None, None, v_meta, meta_slot + parity)
                stage(k_tile(next_blk), 1 - last, v_tile(tile_at(qi)), last, 1 - parity,
                      diagonal=True)
                stage(k_meta(), meta_slot + 1 - parity, None, None, 1 - parity)
                write_block(qi)
                reset_state()

        return carry

    lax.fori_loop(0, n_q, query_block, 0)


def _out_ffn_kernel(x_ref, a_ref, p_ref, wo_ref, n2w_ref, wg_ref, wu_ref, wd_ref, o_ref):
    t = FFN_SUB
    n_sub = x_ref.shape[0] // t

    def mixed(r):
        rows = slice(r * t, (r + 1) * t)
        mix = _dot(jnp.concatenate([a_ref[rows, :], p_ref[rows, :]], axis=1), wo_ref[...])
        h1 = x_ref[rows, :] + mix
        return h1, _bf16(h1 * _rms_scale(h1) * n2w_ref[...])

    nxt = mixed(0)
    for r in range(n_sub):
        h1, n2 = nxt
        if r + 1 < n_sub:
            nxt = mixed(r + 1)
        acc = h1
        lo = 0
        for width in FF_CHUNKS:
            g = _dot(n2, wg_ref[:, lo:lo + width])
            u = _dot(n2, wu_ref[:, lo:lo + width])
            acc = acc + _dot(_bf16(g * jax.nn.sigmoid(g) * u), wd_ref[lo:lo + width, :])
            lo += width
        o_ref[r * t:(r + 1) * t, :] = acc


def _const_spec(shape, single_buffer=False):
    zeros = (0,) * len(shape)
    if single_buffer:
        return pl.BlockSpec(shape, lambda *_: zeros, pipeline_mode=pl.Buffered(1))
    return pl.BlockSpec(shape, lambda *_: zeros)


def _block_diag(blocks):
    n, r, c = blocks.shape
    out = jnp.zeros((n * r, n * c), blocks.dtype)
    for i in range(n):
        out = out.at[i * r:(i + 1) * r, i * c:(i + 1) * c].set(blocks[i])
    return out


def kernel(x, meta_tokens, norm1_w, w_in, b_fgate, q_norm_w, k_norm_w, w_pool, pool_scale, w_out,
           norm2_w, w_gate, w_up, w_down):
    batch, seq, d_model = x.shape
    assert w_in.shape[0] == 1, "one layer"
    assert seq % INPROJ_BLOCK == 0 and INPROJ_BLOCK % INPROJ_SUB == 0
    assert seq % ROW_BLOCK == 0 and seq % Q_BLOCK == 0 and Q_BLOCK == KV_BLOCK
    d_ff = w_gate.shape[-1]
    assert sum(FF_CHUNKS) == d_ff
    f32, bf16 = jnp.float32, jnp.bfloat16

    w = w_in[0]
    w_q, w_k, w_v = (w[:, i * D_ATTN:(i + 1) * D_ATTN] for i in range(3))
    w_fg = w[:, 3 * D_ATTN:3 * D_ATTN + N_HEADS]
    w_u = w[:, 3 * D_ATTN + N_HEADS:]
    wt = jnp.concatenate([w_q.T, w_v.T, w_fg.T, jnp.zeros((GATE_ROWS - N_HEADS, d_model), f32)],
                         axis=0).astype(bf16)
    wku = jnp.concatenate([w_k, w_u], axis=1).astype(bf16)
    n1w = norm1_w[0].reshape(1, d_model)
    n2w = norm2_w[0].reshape(1, d_model)
    bfg = jnp.broadcast_to(b_fgate[0].reshape(N_HEADS, 1), (N_HEADS, LANES))
    qw = jnp.broadcast_to(jnp.tile(q_norm_w[0] * (HEAD_DIM ** -0.5 * LOG2E), N_HEADS).reshape(D_ATTN, 1),
                          (D_ATTN, LANES))
    kw = jnp.tile(k_norm_w[0], N_HEADS).reshape(1, D_ATTN)
    bd = jnp.asarray(np.kron(np.eye(MXU_TILE // HEAD_DIM, dtype=np.float32),
                             np.ones((HEAD_DIM, HEAD_DIM), np.float32)), bf16)
    wp = jnp.stack([_block_diag(w_pool[0, 2 * i:2 * i + 2]) for i in range(D_POOL // MXU_TILE)]
                   ).astype(bf16)
    ps = pool_scale[0].reshape(1, D_POOL)

    def upper_tri(n):
        return jnp.asarray(np.triu(np.ones((n, n), np.float32)), bf16)

    meta_blk = jnp.zeros((META_BLOCK, d_model), f32).at[0:N_META].set(meta_tokens)
    n_proj_t = 2 * D_ATTN + GATE_ROWS

    km, cpm, vtm, um, cm = pl.pallas_call(
        _meta_inproj_kernel,
        out_shape=(jax.ShapeDtypeStruct((META_BLOCK, D_ATTN), bf16),
                   jax.ShapeDtypeStruct((META_BLOCK, LANES), bf16),
                   jax.ShapeDtypeStruct((D_ATTN, META_BLOCK), bf16),
                   jax.ShapeDtypeStruct((N_META, D_POOL), f32),
                   jax.ShapeDtypeStruct((N_HEADS, LANES), f32)),
        compiler_params=pltpu.CompilerParams(vmem_limit_bytes=VMEM_LIMIT_BYTES),
        name="meta_inproj",
    )(meta_blk, n1w, wt, wku, bfg, kw, bd, upper_tri(META_BLOCK))

    tok = lambda b, j: (b, j, 0)
    tok_t = lambda b, j: (b, 0, j)
    qt, k, cp, vt, po = pl.pallas_call(
        _inproj_kernel,
        grid=(batch, seq // INPROJ_BLOCK),
        in_specs=[pl.BlockSpec((None, INPROJ_BLOCK, d_model), tok),
                  _const_spec((N_META, D_POOL)), _const_spec((N_HEADS, LANES)),
                  _const_spec((1, d_model)), _const_spec((n_proj_t, d_model)),
                  _const_spec((d_model, D_ATTN + D_POOL)), _const_spec((N_HEADS, LANES)),
                  _const_spec((D_ATTN, LANES)), _const_spec((1, D_ATTN)),
                  _const_spec((MXU_TILE, MXU_TILE)), _const_spec((INPROJ_SUB, INPROJ_SUB)),
                  _const_spec((D_POOL // MXU_TILE, MXU_TILE, MXU_TILE)), _const_spec((1, D_POOL))],
        out_specs=(pl.BlockSpec((None, D_ATTN, INPROJ_BLOCK), tok_t),
                   pl.BlockSpec((None, INPROJ_BLOCK, D_ATTN), tok),
                   pl.BlockSpec((None, INPROJ_BLOCK, LANES), tok),
                   pl.BlockSpec((None, D_ATTN, INPROJ_BLOCK), tok_t),
                   pl.BlockSpec((None, INPROJ_BLOCK, D_POOL), tok)),
        out_shape=(jax.ShapeDtypeStruct((batch, D_ATTN, seq), bf16),
                   jax.ShapeDtypeStruct((batch, seq, D_ATTN), bf16),
                   jax.ShapeDtypeStruct((batch, seq, LANES), bf16),
                   jax.ShapeDtypeStruct((batch, D_ATTN, seq), bf16),
                   jax.ShapeDtypeStruct((batch, seq, D_POOL), bf16)),
        scratch_shapes=[pltpu.VMEM((N_META + INPROJ_BLOCK, D_POOL), f32),
                        pltpu.VMEM((N_HEADS, LANES), f32)],
        compiler_params=pltpu.CompilerParams(
            dimension_semantics=("arbitrary", "arbitrary"), vmem_limit_bytes=VMEM_LIMIT_BYTES),
        name="inproj",
    )(x, um, cm, n1w, wt, wku, bfg, qw, kw, bd, upper_tri(INPROJ_SUB), wp, ps)

    n_pairs = N_HEADS // HEADS_PER_STEP
    ffn_weights = (w_out[0], w_gate[0], w_up[0], w_down[0])
    n_steps = batch * n_pairs
    slab = lambda b, p: (b * n_pairs + p, 0)
    slab_specs = []
    for wgt in ffn_weights:
        assert wgt.shape[0] % (n_steps * 16) == 0, "row slabs must be whole bf16 sublane tiles"
        slab_specs.append(pl.BlockSpec((wgt.shape[0] // n_steps, wgt.shape[1]), slab))
    a, wo, wg, wu, wd = pl.pallas_call(
        _attn_kernel,
        grid=(batch, n_pairs),
        in_specs=[pl.BlockSpec((None, LANES, seq), lambda b, p: (b, p, 0)),
                  pl.BlockSpec((None, seq, LANES), lambda b, p: (b, 0, p)),
                  pl.BlockSpec((None, seq, LANES), lambda b, p: (b, 0, 0)),
                  pl.BlockSpec((None, LANES, seq), lambda b, p: (b, p, 0)),
                  pl.BlockSpec((META_BLOCK, LANES), lambda b, p: (0, p)),
                  pl.BlockSpec((META_BLOCK, LANES), lambda b, p: (0, 0)),
                  pl.BlockSpec((LANES, META_BLOCK), lambda b, p: (p, 0))] + slab_specs,
        out_specs=[pl.BlockSpec((None, seq, LANES), lambda b, p: (b, 0, p))] + slab_specs,
        out_shape=[jax.ShapeDtypeStruct((batch, seq, D_ATTN), bf16)]
        + [jax.ShapeDtypeStruct(wgt.shape, bf16) for wgt in ffn_weights],
        scratch_shapes=[pltpu.VMEM((2, HEADS_PER_STEP, 2 * LANES, Q_BLOCK), bf16),
                        pltpu.VMEM((4, HEADS_PER_STEP, KV_BLOCK, Q_BLOCK), f32),
                        pltpu.VMEM((4, HEADS_PER_STEP, 1, Q_BLOCK), f32),
                        pltpu.VMEM((HEADS_PER_STEP, 1, Q_BLOCK), f32),
                        pltpu.VMEM((HEADS_PER_STEP, HEAD_DIM + ONES_ROWS, Q_BLOCK), f32)],
        compiler_params=pltpu.CompilerParams(
            dimension_semantics=("parallel", "parallel"), vmem_limit_bytes=VMEM_LIMIT_BYTES),
        name="fox_attn",
    )(qt, k, cp, vt, km, cpm, vtm, *ffn_weights)

    rows = batch * seq
    row = lambda r: (r, 0)
    out = pl.pallas_call(
        _out_ffn_kernel,
        grid=(rows // ROW_BLOCK,),
        in_specs=[pl.BlockSpec((ROW_BLOCK, d_model), row),
                  pl.BlockSpec((ROW_BLOCK, D_ATTN), row),
                  pl.BlockSpec((ROW_BLOCK, D_POOL), row),
                  _const_spec((D_ATTN + D_POOL, d_model), True), _const_spec((1, d_model)),
                  _const_spec((d_model, d_ff), True), _const_spec((d_model, d_ff), True),
                  _const_spec((d_ff, d_model), True)],
        out_specs=pl.BlockSpec((ROW_BLOCK, d_model), row),
        out_shape=jax.ShapeDtypeStruct((rows, d_model), f32),
        compiler_params=pltpu.CompilerParams(
            dimension_semantics=("parallel",), vmem_limit_bytes=VMEM_LIMIT_BYTES),
        name="out_ffn",
    )(x.reshape(rows, d_model), a.reshape(rows, D_ATTN), po.reshape(rows, D_POOL),
      wo, n2w, wg, wu, wd)
    return out.reshape(batch, seq, d_model)
```
